```python
import math
import jax, jax.numpy as jnp
from jax import lax
import numpy as np

D_MODEL = 2048
BATCH = 2
SEQ = 16384
DEPTH = 4

MIX_WIDTH = D_MODEL
C_RWKV = MIX_WIDTH // 2
RWKV_HEAD_SIZE = 64
N_RWKV_HEADS = C_RWKV // RWKV_HEAD_SIZE
C_DIFF = MIX_WIDTH - C_RWKV
DIFF_HEAD_DIM = 64
N_DIFF_HEADS = C_DIFF // (2 * DIFF_HEAD_DIM)
Q_BLOCK = 128
D_DECAY_LORA = 64
D_AAA_LORA = 64
D_MV_LORA = 32
D_GATE_LORA = 160
SHIFT_COLS = 3 * C_RWKV + D_DECAY_LORA + D_AAA_LORA + D_GATE_LORA
C_IN = SHIFT_COLS + 3 * C_DIFF
RWKV_SPLITS = [C_RWKV, 2 * C_RWKV, 3 * C_RWKV, 3 * C_RWKV + D_DECAY_LORA, 3 * C_RWKV + D_DECAY_LORA + D_AAA_LORA]
D_FF_DENSE = 256 * ((8 * D_MODEL // 3 + 255) // 256)
N_EXPERTS = 8
TOP_K = 2
D_FF_EXPERT = 7 * D_MODEL // 2
EXPERT_BLOCK = 256
D_PLE = 256
N_DENSE = (DEPTH + 1) // 2
N_MOE = DEPTH // 2
RMS_EPS = 1e-6
GN_EPS = 64e-5
SUBLN_EPS = 1e-5

kernel_name = 'hybrid_rwkv7_diffattn_moe_block'


def rmsnorm(x, g, eps=RMS_EPS):
    xf = x.astype(jnp.float32)
    y = xf * lax.rsqrt(jnp.mean(xf * xf, axis=-1, keepdims=True) + eps)
    return y.astype(x.dtype) * g


def token_shift(z, mu):
    prev = jnp.pad(z, ((0, 0), (1, 0), (0, 0)))[:, :-1]
    return z + mu * (prev - z)


def wkv7_scan(r, w, k, v, a, b):
    Bn, Sn, Hn, Nn = r.shape
    def step(state, inp):
        r_t, w_t, k_t, v_t, a_t, b_t = inp
        sa = jnp.einsum('bhvk,bhk->bhv', state, a_t)
        state = state * w_t[:, :, None, :] + sa[..., None] * b_t[:, :, None, :] + v_t[..., None] * k_t[:, :, None, :]
        return state, jnp.einsum('bhvk,bhk->bhv', state, r_t)
    s0 = jnp.zeros((Bn, Hn, Nn, Nn), jnp.float32)
    xs = tuple(jnp.moveaxis(t, 1, 0) for t in (r, w, k, v, a, b))
    _, y = lax.scan(step, s0, xs)
    return jnp.moveaxis(y, 0, 1)


def rwkv7_time_mix(r, k, v, lw, la, lg, w0, w2, a0, a2, g2, k_k, k_a, r_k, lnx_w, lnx_b):
    f32 = jnp.float32
    Bn, Sn, _ = r.shape
    heads = lambda t: t.reshape(Bn, Sn, N_RWKV_HEADS, RWKV_HEAD_SIZE)
    r, k, v = r.astype(f32), k.astype(f32), v.astype(f32)
    w_log = -jax.nn.softplus(-(w0 + jnp.tanh(lw) @ w2).astype(f32)) - 0.5
    decay = jnp.exp(-jnp.exp(w_log))
    a = jax.nn.sigmoid((a0 + la @ a2).astype(f32))
    g = jax.nn.sigmoid(lg) @ g2
    kk = heads(k * k_k)
    kk = kk / jnp.maximum(jnp.sqrt(jnp.sum(kk * kk, axis=-1, keepdims=True)), 1e-12)
    k = k * (1.0 + (a - 1.0) * k_a)
    r_h, k_h, v_h, a_h = heads(r), heads(k), heads(v), heads(a)
    y = wkv7_scan(r_h, heads(decay), k_h, v_h, -kk, kk * a_h)
    mean = jnp.mean(y, axis=-1, keepdims=True)
    var = jnp.mean(jnp.square(y - mean), axis=-1, keepdims=True)
    y = ((y - mean) * lax.rsqrt(var + GN_EPS)).reshape(Bn, Sn, C_RWKV) * lnx_w + lnx_b
    bonus = jnp.sum(r_h * k_h * r_k, axis=-1, keepdims=True) * v_h
    y = y + bonus.reshape(Bn, Sn, C_RWKV)
    return (y * g.astype(f32)).astype(lw.dtype)


def lambda_init(layer_idx):
    return 0.8 - 0.6 * math.exp(-0.3 * layer_idx)


def diff_attention(q, k, v, lam_params, subln_g, lam_init):
    Bn, Sn, _ = q.shape
    H, d = N_DIFF_HEADS, DIFF_HEAD_DIM
    nb = Sn // Q_BLOCK
    lp = lam_params.astype(jnp.float32)
    lam = jnp.exp(jnp.sum(lp[0] * lp[1])) - jnp.exp(jnp.sum(lp[2] * lp[3])) + lam_init
    qb = q.reshape(Bn, nb, Q_BLOCK, H, 2, d).transpose(1, 0, 3, 4, 2, 5)
    kt = k.reshape(Bn, Sn, H, 2, d).transpose(0, 2, 3, 1, 4)
    vt = v.reshape(Bn, Sn, H, 2 * d).transpose(0, 2, 1, 3)
    key_pos = jnp.arange(Sn)
    scale = d ** -0.5
    def block(args):
        q_blk, bi = args
        s = jnp.einsum('bhcqd,bhckd->bhcqk', q_blk, kt).astype(jnp.float32) * scale
        q_pos = bi * Q_BLOCK + jnp.arange(Q_BLOCK)
        s = jnp.where(key_pos[None, :] <= q_pos[:, None], s, -jnp.inf)
        pr = jax.nn.softmax(s, axis=-1)
        amap = pr[:, :, 0] - lam * pr[:, :, 1]
        return jnp.einsum('bhqk,bhkv->bhqv', amap.astype(vt.dtype), vt)
    o = lax.map(block, (qb, jnp.arange(nb)))
    o = o.transpose(1, 0, 3, 2, 4).reshape(Bn, Sn, H, 2 * d)
    of = o.astype(jnp.float32)
    of = of * lax.rsqrt(jnp.mean(of * of, axis=-1, keepdims=True) + SUBLN_EPS)
    o = of.astype(q.dtype) * subln_g * (1.0 - lam_init)
    return o.reshape(Bn, Sn, C_DIFF)


def swiglu(u, w1, w3, w2):
    return (jax.nn.silu(u @ w1) * (u @ w3)) @ w2


def moe_swiglu(u, router_w, w1, w3, w2):
    Bn, Sn, Dn = u.shape
    xt = u.reshape(-1, Dn)
    n_tok = xt.shape[0]
    m = n_tok * TOP_K
    logits = (xt @ router_w).astype(jnp.float32)
    top_logit, top_idx = lax.top_k(logits, TOP_K)
    gates = jax.nn.softmax(top_logit, axis=-1)
    e_flat = top_idx.reshape(-1)
    order = jnp.argsort(e_flat, stable=True)
    e_sorted = e_flat[order]
    tok_sorted = order // TOP_K
    gate_sorted = gates.reshape(-1)[order]
    counts = jnp.bincount(e_flat, length=N_EXPERTS)
    starts = jnp.cumsum(counts) - counts
    padded = (counts + EXPERT_BLOCK - 1) // EXPERT_BLOCK * EXPERT_BLOCK
    pad_ends = jnp.cumsum(padded)
    pad_starts = pad_ends - padded
    dest = pad_starts[e_sorted] + jnp.arange(m) - starts[e_sorted]
    n_blocks = -(-m // EXPERT_BLOCK) + N_EXPERTS
    x_pad = jnp.zeros((n_blocks * EXPERT_BLOCK, Dn), u.dtype).at[dest].set(xt[tok_sorted])
    blk_start = jnp.arange(n_blocks) * EXPERT_BLOCK
    blk_expert = jnp.minimum(jnp.sum(blk_start[:, None] >= pad_ends[None, :], axis=1), N_EXPERTS - 1)
    def expert_block(args):
        xb, e = args
        return (jax.nn.silu(xb @ w1[e]) * (xb @ w3[e])) @ w2[e]
    y_pad = lax.map(expert_block, (x_pad.reshape(n_blocks, EXPERT_BLOCK, Dn), blk_expert)).reshape(-1, Dn)
    y_slot = y_pad[dest] * gate_sorted[:, None].astype(u.dtype)
    y = jax.ops.segment_sum(y_slot, tok_sorted, num_segments=n_tok)
    return y.reshape(Bn, Sn, Dn)


def setup_inputs(seed: int = 0) -> dict:
    key = jax.random.key(seed)
    keys = iter(jax.random.split(key, 48))
    f32 = jnp.float32
    D = D_MODEL
    def nrm(shape, scale):
        return jax.random.normal(next(keys), shape, f32) * scale
    def near(shape, center, noise):
        return center + noise * jax.random.normal(next(keys), shape, f32)
    def unif(shape):
        return jax.random.uniform(next(keys), shape, f32)
    decay_base = -6.5 + 5.0 * (jnp.arange(C_RWKV, dtype=f32) / (C_RWKV - 1)) ** 0.85
    return {
        'x': nrm((BATCH, SEQ, D), 1.0),
        'p': nrm((DEPTH, BATCH, SEQ, D_PLE), 1.0),
        'g_mix': near((DEPTH, D), 1.0, 0.02),
        'w_in': nrm((DEPTH, D, C_IN), D ** -0.5),
        'w_vres_down': nrm((DEPTH - 1, D, D_MV_LORA), D ** -0.5),
        'mu_shift': unif((DEPTH, SHIFT_COLS)),
        'mu_vres': unif((DEPTH - 1, D_MV_LORA)),
        'rw_w0': decay_base + nrm((DEPTH, C_RWKV), 0.1),
        'rw_w2': nrm((DEPTH, D_DECAY_LORA, C_RWKV), 0.1 * D_DECAY_LORA ** -0.5),
        'rw_a0': nrm((DEPTH, C_RWKV), 0.1),
        'rw_a2': nrm((DEPTH, D_AAA_LORA, C_RWKV), D_AAA_LORA ** -0.5),
        'rw_g2': nrm((DEPTH, D_GATE_LORA, C_RWKV), D_GATE_LORA ** -0.5),
        'rw_v0': near((DEPTH - 1, C_RWKV), 1.0, 0.1),
        'rw_v2': nrm((DEPTH - 1, D_MV_LORA, C_RWKV), D_MV_LORA ** -0.5),
        'rw_k_k': near((DEPTH, C_RWKV), 0.85, 0.05),
        'rw_k_a': near((DEPTH, C_RWKV), 1.0, 0.05),
        'rw_r_k': nrm((DEPTH, N_RWKV_HEADS, RWKV_HEAD_SIZE), 0.1),
        'rw_lnx_w': near((DEPTH, C_RWKV), 1.0, 0.02),
        'rw_lnx_b': nrm((DEPTH, C_RWKV), 0.02),
        'diff_lambda': nrm((DEPTH, 4, DIFF_HEAD_DIM), 0.1),
        'diff_subln': near((DEPTH, 2 * DIFF_HEAD_DIM), 1.0, 0.02),
        'w_out': nrm((DEPTH, MIX_WIDTH, D), MIX_WIDTH ** -0.5),
        'g_ffn': near((DEPTH, D), 1.0, 0.02),
        'dense_w1': nrm((N_DENSE, D, D_FF_DENSE), D ** -0.5),
        'dense_w3': nrm((N_DENSE, D, D_FF_DENSE), D ** -0.5),
        'dense_w2': nrm((N_DENSE, D_FF_DENSE, D), D_FF_DENSE ** -0.5),
        'router_w': nrm((N_MOE, D, N_EXPERTS), D ** -0.5),
        'moe_w1': nrm((N_MOE, N_EXPERTS, D, D_FF_EXPERT), D ** -0.5),
        'moe_w3': nrm((N_MOE, N_EXPERTS, D, D_FF_EXPERT), D ** -0.5),
        'moe_w2': nrm((N_MOE, N_EXPERTS, D_FF_EXPERT, D), D_FF_EXPERT ** -0.5),
        'g_ple': near((DEPTH, D), 1.0, 0.02),
        'w_ple_gate': nrm((DEPTH, D, D), D ** -0.5),
        'w_ple_proj': nrm((DEPTH, D_PLE, D), D_PLE ** -0.5),
        'g_final': near((D,), 1.0, 0.02),
    }


def reference(x, p, g_mix, w_in, w_vres_down, mu_shift, mu_vres, rw_w0, rw_w2, rw_a0, rw_a2, rw_g2,
              rw_v0, rw_v2, rw_k_k, rw_k_a, rw_r_k, rw_lnx_w, rw_lnx_b, diff_lambda, diff_subln, w_out,
              g_ffn, dense_w1, dense_w3, dense_w2, router_w, moe_w1, moe_w3, moe_w2,
              g_ple, w_ple_gate, w_ple_proj, g_final):
    h = x
    v_first = None
    for i in range(DEPTH):
        u = rmsnorm(h, g_mix[i])
        if i == 0:
            z = u @ w_in[0]
            v_gate = None
        else:
            z = u @ jnp.concatenate([w_in[i], w_vres_down[i - 1]], axis=1)
            lv = token_shift(z[..., C_IN:], mu_vres[i - 1])
            v_gate = jax.nn.sigmoid(rw_v0[i - 1] + lv @ rw_v2[i - 1])
        zs = token_shift(z[..., :SHIFT_COLS], mu_shift[i])
        r, k, v, lw, la, lg = jnp.split(zs, RWKV_SPLITS, axis=-1)
        if v_first is None:
            v_first = v
        else:
            v = v + (v_first - v) * v_gate
        y_rwkv = rwkv7_time_mix(r, k, v, lw, la, lg, rw_w0[i], rw_w2[i], rw_a0[i], rw_a2[i], rw_g2[i],
                                rw_k_k[i], rw_k_a[i], rw_r_k[i], rw_lnx_w[i], rw_lnx_b[i])
        q_d, k_d, v_d = jnp.split(z[..., SHIFT_COLS:C_IN], [C_DIFF, 2 * C_DIFF], axis=-1)
        y_diff = diff_attention(q_d, k_d, v_d, diff_lambda[i], diff_subln[i], lambda_init(i))
        h = h + jnp.concatenate([y_rwkv, y_diff], axis=-1) @ w_out[i]
        u = rmsnorm(h, g_ffn[i])
        if i % 2 == 0:
            h = h + swiglu(u, dense_w1[i // 2], dense_w3[i // 2], dense_w2[i // 2])
        else:
            h = h + moe_swiglu(u, router_w[i // 2], moe_w1[i // 2], moe_w3[i // 2], moe_w2[i // 2])
        u = rmsnorm(h, g_ple[i])
        h = h + jax.nn.sigmoid(u @ w_ple_gate[i]) * (p[i] @ w_ple_proj[i])
    return rmsnorm(h, g_final)
```

```python
import functools
import math

import jax
import jax.numpy as jnp
from jax import lax
from jax.experimental import pallas as pl
from jax.experimental.pallas import tpu as pltpu

F32 = jnp.float32
BF16 = jnp.bfloat16

LANES = 128
VMEM_LIMIT = 56 * 1024 * 1024

RWKV_HEAD = 64
HEADS_PER_TILE = LANES // RWKV_HEAD
DIFF_HEAD_DIM = 64
D_DECAY_LORA, D_AAA_LORA, D_MV_LORA, D_GATE_LORA = 64, 64, 32, 160
LORA_COLS = 3 * LANES
TOP_K = 2
RMS_EPS = 1e-6
GN_EPS = 64e-5
SUBLN_EPS = 1e-5

WKV_CHUNK = 64
WKV_PASSES = 3


def _cparams(sem):
    return pltpu.CompilerParams(dimension_semantics=sem, vmem_limit_bytes=VMEM_LIMIT)


def _dot01_left(m01, x, nsplit=3):
    acc, rem = None, x
    for _ in range(nsplit):
        piece = rem.astype(BF16)
        t = jnp.dot(m01, piece, preferred_element_type=F32)
        acc = t if acc is None else acc + t
        rem = rem - piece.astype(F32)
    return acc


def _dot01_right(x, m01, nsplit=3):
    acc, rem = None, x
    for _ in range(nsplit):
        piece = rem.astype(BF16)
        t = jnp.dot(piece, m01, preferred_element_type=F32)
        acc = t if acc is None else acc + t
        rem = rem - piece.astype(F32)
    return acc


def _split2(x):
    hi = x.astype(BF16)
    return hi, (x - hi.astype(F32)).astype(BF16)


def _mm(a, b, dims):
    dn = (dims, ((), ()))
    if WKV_PASSES == 1:
        return lax.dot_general(a.astype(BF16), b.astype(BF16), dn, preferred_element_type=F32)
    a_hi, a_lo = _split2(a)
    b_hi, b_lo = _split2(b)
    out = lax.dot_general(a_hi, b_hi, dn, preferred_element_type=F32)
    out = out + lax.dot_general(a_hi, b_lo, dn, preferred_element_type=F32)
    return out + lax.dot_general(a_lo, b_hi, dn, preferred_element_type=F32)


def _sdot(a, b):
    return _mm(a, b, ((1,), (0,)))


def _sdot_nt(a, b):
    return _mm(a, b, ((1,), (1,)))


def _sdot_tn(a, b):
    return _mm(a, b, ((0,), (0,)))


def _token_shift(x, prev_row, mu):
    rolled = pltpu.roll(x, 1, 0)
    row = lax.broadcasted_iota(jnp.int32, x.shape, 0)
    prev = jnp.where(row == 0, prev_row, rolled)
    return x + mu * (prev - x)


def _sigmoid(x):
    return 1.0 / (1.0 + jnp.exp(-x))


def _wkv_body(first_layer, tb, *refs):
    if first_layer:
        (zr, zk, zv, zl, par, mul, w2, a2, g2, v2,
         y_out, vf_out, p_s, prev_s, prevl_s, st_s, y_s) = refs
        vf_in = None
    else:
        (zr, zk, zv, zl, par, mul, w2, a2, g2, v2, vf_in,
         y_out, p_s, prev_s, prevl_s, st_s, y_s) = refs
    L = WKV_CHUNK
    n_chunks = tb // L

    @pl.when(pl.program_id(2) == 0)
    def _():
        p_s[...] = jnp.zeros_like(p_s)
        prev_s[...] = jnp.zeros_like(prev_s)
        prevl_s[...] = jnp.zeros_like(prevl_s)

    lane = lax.broadcasted_iota(jnp.int32, (LANES, LANES), 1)
    sub = lax.broadcasted_iota(jnp.int32, (LANES, LANES), 0)
    same_head = (lane // RWKV_HEAD) == (sub // RWKV_HEAD)
    block_ones = jnp.where(same_head, 1.0, 0.0).astype(BF16)

    zr_v, zk_v, zv_v, zl_v = zr[...], zk[...], zv[...], zl[...]
    r = _token_shift(zr_v, prev_s[0:1, :], par[0:1, :])
    k = _token_shift(zk_v, prev_s[1:2, :], par[1:2, :])
    v = _token_shift(zv_v, prev_s[2:3, :], par[2:3, :])
    ls = _token_shift(zl_v, prevl_s[...], mul[...])
    prev_s[0:1, :] = zr_v[tb - 1:tb, :]
    prev_s[1:2, :] = zk_v[tb - 1:tb, :]
    prev_s[2:3, :] = zv_v[tb - 1:tb, :]
    prevl_s[...] = zl_v[tb - 1:tb, :]

    l0 = ls[:, 0:LANES]
    l1 = ls[:, LANES:3 * LANES]
    l2 = ls[:, 2 * LANES:3 * LANES]
    wl = par[3:4, :] + jnp.dot(jnp.tanh(l0).astype(BF16), w2[...], preferred_element_type=F32)
    al = par[4:5, :] + jnp.dot(l0.astype(BF16), a2[...], preferred_element_type=F32)
    gate = jnp.dot(_sigmoid(l1).astype(BF16), g2[...], preferred_element_type=F32)
    if first_layer:
        vf_out[...] = v
    else:
        vg = _sigmoid(par[5:6, :] + jnp.dot(l2.astype(BF16), v2[...], preferred_element_type=F32))
        v = v + (vf_in[...] - v) * vg
    neg_wl = -wl
    softplus = jnp.maximum(neg_wl, 0.0) + jnp.log(1.0 + jnp.exp(-jnp.abs(neg_wl)))
    logw = -jnp.exp(-softplus - 0.5)
    asig = _sigmoid(al)
    kk = k * par[6:7, :]
    kk_norm = jnp.sqrt(_dot01_right(kk * kk, block_ones))
    kk = kk / jnp.maximum(kk_norm, 1e-12)
    k2 = k * (1.0 + (asig - 1.0) * par[7:8, :])
    bonus = _dot01_right(r * k2 * par[8:9, :], block_ones) * v

    st_s[0] = r
    st_s[1] = logw
    st_s[2] = k2
    st_s[3] = v
    st_s[4] = -kk
    st_s[5] = kk * asig
    st_s[6] = gate
    st_s[7] = bonus

    ti = lax.broadcasted_iota(jnp.int32, (L, L), 0)
    si = lax.broadcasted_iota(jnp.int32, (L, L), 1)
    strict = si < ti
    incl = si <= ti
    tri01 = jnp.where(incl, 1.0, 0.0).astype(BF16)
    eye = jnp.where(si == ti, 1.0, 0.0).astype(F32)
    lane_row = lax.broadcasted_iota(jnp.int32, (L, LANES), 1)
    head_lanes = [(lane_row // RWKV_HEAD) == h for h in range(HEADS_PER_TILE)]
    diag = lane == sub

    def chunk(c, p):
        sl = pl.ds(pl.multiple_of(c * L, L), L)
        r_c, lw_c, k_c, v_c, a_c, b_c = (st_s[i, sl, :] for i in range(6))
        cum = _dot01_left(tri01, lw_c)
        c_last = cum[L - 1:L, :]
        e_inv = jnp.exp(-cum)
        e_end = jnp.exp(c_last - cum)
        a_t = a_c * jnp.exp(cum - lw_c)
        b_t = b_c * e_inv
        k_t = k_c * e_inv
        r_t = r_c * jnp.exp(cum)
        b_h = b_c * e_end
        k_h = k_c * e_end
        w_end = jnp.exp(c_last)

        ta = jnp.zeros((L, LANES), F32)
        u0 = jnp.zeros((L, LANES), F32)
        rp = jnp.zeros((L, LANES), F32)
        y1 = jnp.zeros((L, LANES), F32)
        for h in range(HEADS_PER_TILE):
            hm = head_lanes[h]
            a_m = jnp.where(hm, a_t, 0.0)
            r_m = jnp.where(hm, r_t, 0.0)
            a_ab = jnp.where(strict, _sdot_nt(a_m, b_t), 0.0)
            a_ak = jnp.where(strict, _sdot_nt(a_m, k_t), 0.0)
            a_rb = jnp.where(incl, _sdot_nt(r_m, b_t), 0.0)
            a_rk = jnp.where(incl, _sdot_nt(r_m, k_t), 0.0)
            t_inv = eye + a_ab
            pw = _sdot(a_ab, a_ab)
            steps = int(math.log2(L)) - 1
            for j in range(steps):
                if j + 1 < steps:
                    both = _sdot(jnp.concatenate([t_inv, pw], axis=0), pw)
                    t_inv = t_inv + both[:L]
                    pw = both[L:]
                else:
                    t_inv = t_inv + _sdot(t_inv, pw)
            x = _sdot(a_ak, v_c)
            ta_h = _sdot(t_inv, a_t)
            u0_h = _sdot(t_inv, x)
            rp_h = r_t + _sdot(a_rb, ta_h)
            y1_h = _sdot(a_rb, u0_h) + _sdot(a_rk, v_c)
            ta = jnp.where(hm, ta_h, ta)
            u0 = jnp.where(hm, u0_h, u0)
            rp = jnp.where(hm, rp_h, rp)
            y1 = jnp.where(hm, y1_h, y1)
        m = _sdot_tn(b_h, ta) + jnp.where(diag, w_end, 0.0)
        n = _sdot_tn(b_h, u0) + _sdot_tn(k_h, v_c)
        m = jnp.where(same_head, m, 0.0)
        n = jnp.where(same_head, n, 0.0)
        y_s[sl, :] = _sdot(rp, p) + y1
        return _sdot(m, p) + n

    p_s[...] = lax.fori_loop(0, n_chunks, chunk, p_s[...])

    y = y_s[...]
    inv_n = 1.0 / RWKV_HEAD
    mean = _dot01_right(y, block_ones) * inv_n
    d = y - mean
    var = _dot01_right(d * d, block_ones) * inv_n
    yn = d * lax.rsqrt(var + GN_EPS) * par[9:10, :] + par[10:11, :]
    y_out[...] = ((yn + st_s[7]) * st_s[6]).astype(y_out.dtype)


def rwkv_time_mix(z_r, par, mul, w2p, a2p, g2p, v2p, v_first, *, tb):
    bsz, seq, cols = z_r.shape
    c_rwkv = (cols - LORA_COLS) // 3
    n_tiles = c_rwkv // LANES
    first = v_first is None
    grid = (bsz, n_tiles, seq // tb)
    col_spec = lambda off: pl.BlockSpec((None, tb, LANES), lambda b, h, t: (b, t, off + h))
    par_rows = par.shape[0]
    in_specs = [
        col_spec(0), col_spec(n_tiles), col_spec(2 * n_tiles),
        pl.BlockSpec((None, tb, LORA_COLS), lambda b, h, t: (b, t, 3 * c_rwkv // LORA_COLS)),
        pl.BlockSpec((par_rows, LANES), lambda b, h, t: (0, h)),
        pl.BlockSpec((1, LORA_COLS), lambda b, h, t: (0, 0)),
        pl.BlockSpec((LANES, LANES), lambda b, h, t: (0, h)),
        pl.BlockSpec((LANES, LANES), lambda b, h, t: (0, h)),
        pl.BlockSpec((2 * LANES, LANES), lambda b, h, t: (0, h)),
        pl.BlockSpec((LANES, LANES), lambda b, h, t: (0, h)),
    ]
    args = [z_r, z_r, z_r, z_r, par, mul, w2p, a2p, g2p, v2p]
    y_spec = pl.BlockSpec((None, tb, LANES), lambda b, h, t: (b, t, h))
    y_shape = jax.ShapeDtypeStruct((bsz, seq, c_rwkv), BF16)
    if first:
        out_shape = (y_shape, jax.ShapeDtypeStruct((bsz, seq, c_rwkv), F32))
        out_specs = (y_spec, y_spec)
    else:
        in_specs.append(y_spec)
        args.append(v_first)
        out_shape = y_shape
        out_specs = y_spec
    scratch = [
        pltpu.VMEM((LANES, LANES), F32),
        pltpu.VMEM((8, LANES), F32),
        pltpu.VMEM((1, LORA_COLS), F32),
        pltpu.VMEM((8, tb, LANES), F32),
        pltpu.VMEM((tb, LANES), F32),
    ]
    return pl.pallas_call(
        functools.partial(_wkv_body, first, tb),
        grid=grid, in_specs=in_specs, out_specs=out_specs, out_shape=out_shape,
        scratch_shapes=scratch, name="rwkv7_time_mix",
        compiler_params=_cparams(("parallel", "parallel", "arbitrary")),
    )(*args)


def _rwkv_params(i, mu_shift, mu_vres, rw_w0, rw_w2, rw_a0, rw_a2, rw_g2, rw_v0, rw_v2,
                 rw_k_k, rw_k_a, rw_r_k, rw_lnx_w, rw_lnx_b):
    c = rw_w0.shape[1]
    mu = mu_shift[i]
    zero_c = jnp.zeros((c,), F32)
    rows = [mu[:c], mu[c:2 * c], mu[2 * c:3 * c], rw_w0[i], rw_a0[i],
            rw_v0[i - 1] if i > 0 else zero_c, rw_k_k[i], rw_k_a[i], rw_r_k[i].reshape(-1),
            rw_lnx_w[i], rw_lnx_b[i]]
    par = jnp.stack(rows + [zero_c] * (16 - len(rows)))
    n_lora = D_DECAY_LORA + D_AAA_LORA + D_GATE_LORA
    mu_v = mu_vres[i - 1] if i > 0 else jnp.zeros((D_MV_LORA,), F32)
    mul = jnp.concatenate([mu[3 * c:], mu_v, jnp.zeros((LORA_COLS - n_lora - D_MV_LORA,), F32)])[None, :]
    w2p = jnp.zeros((LANES, c), F32).at[:D_DECAY_LORA].set(rw_w2[i])
    a2p = jnp.zeros((LANES, c), F32).at[D_DECAY_LORA:D_DECAY_LORA + D_AAA_LORA].set(rw_a2[i])
    g2p = jnp.zeros((2 * LANES, c), F32).at[:D_GATE_LORA].set(rw_g2[i])
    v2p = jnp.zeros((LANES, c), F32)
    if i > 0:
        lv0 = n_lora - 2 * LANES
        v2p = v2p.at[lv0:lv0 + D_MV_LORA].set(rw_v2[i - 1])
    return par, mul, w2p.astype(BF16), a2p.astype(BF16), g2p.astype(BF16), v2p.astype(BF16)


def _rmsnorm_rows(x, g):
    ms = jnp.mean(x * x, axis=-1, keepdims=True)
    return x * lax.rsqrt(ms + RMS_EPS) * g


def _norm_matmul_body(x_ref, g_ref, w_ref, o_ref, u_s):
    @pl.when(pl.program_id(1) == 0)
    def _():
        u_s[...] = _rmsnorm_rows(x_ref[...], g_ref[...]).astype(BF16)

    o_ref[...] = jnp.dot(u_s[...], w_ref[...], preferred_element_type=F32).astype(o_ref.dtype)


def norm_matmul(x, g, w, out_dtype, *, tm, tn):
    n, d = x.shape
    c = w.shape[1]
    return pl.pallas_call(
        _norm_matmul_body,
        grid=(n // tm, c // tn),
        in_specs=[pl.BlockSpec((tm, d), lambda i, j: (i, 0)),
                  pl.BlockSpec((1, d), lambda i, j: (0, 0)),
                  pl.BlockSpec((d, tn), lambda i, j: (0, j))],
        out_specs=pl.BlockSpec((tm, tn), lambda i, j: (i, j)),
        out_shape=jax.ShapeDtypeStruct((n, c), out_dtype),
        scratch_shapes=[pltpu.VMEM((tm, d), BF16)],
        name="norm_matmul",
        compiler_params=_cparams(("parallel", "arbitrary")),
    )(x, g, w)


NEG_BIG = -1e30


def _diff_attn_body(tq, lam_init, lp_ref, g_ref, q_ref, k_ref, v_ref, o_ref):
    qi = pl.program_id(2)
    d = DIFF_HEAD_DIM
    lp = lp_ref[...]
    lam = (jnp.exp(jnp.sum(lp[0:1] * lp[1:2], axis=-1, keepdims=True))
           - jnp.exp(jnp.sum(lp[2:3] * lp[3:4], axis=-1, keepdims=True)) + lam_init)
    q = q_ref[...] * jnp.asarray(d ** -0.5, BF16)
    lane = lax.broadcasted_iota(jnp.int32, q.shape, 1)
    q_maps = (jnp.where(lane < d, q, jnp.zeros_like(q)), jnp.where(lane >= d, q, jnp.zeros_like(q)))
    nt = (((1,), (1,)), ((), ()))

    def step(j, carry, masked):
        start = pl.multiple_of(j * tq, tq)
        kb = k_ref[pl.ds(start, tq), :]
        vb = v_ref[pl.ds(start, tq), :]
        new = []
        for c in range(2):
            m, l, acc = carry[c]
            s = lax.dot_general(q_maps[c], kb, nt, preferred_element_type=F32)
            if masked:
                row = lax.broadcasted_iota(jnp.int32, s.shape, 0)
                col = lax.broadcasted_iota(jnp.int32, s.shape, 1)
                s = jnp.where(col <= row, s, NEG_BIG)
            m_new = jnp.maximum(m, jnp.max(s, axis=-1, keepdims=True))
            alpha = jnp.exp(m - m_new)
            pr = jnp.exp(s - m_new)
            l_new = alpha * l + jnp.sum(pr, axis=-1, keepdims=True)
            acc_new = alpha * acc + jnp.dot(pr.astype(BF16), vb, preferred_element_type=F32)
            new.append((m_new, l_new, acc_new))
        return tuple(new)

    init_one = (jnp.full((tq, 1), NEG_BIG, F32), jnp.zeros((tq, 1), F32), jnp.zeros((tq, LANES), F32))
    carry = lax.fori_loop(0, qi, lambda j, c: step(j, c, False), (init_one, init_one))
    (_, l1, acc1), (_, l2, acc2) = step(qi, carry, True)
    o = acc1 / l1 - lam * (acc2 / l2)
    o = o * lax.rsqrt(jnp.mean(o * o, axis=-1, keepdims=True) + SUBLN_EPS)
    o_ref[...] = (o * g_ref[...] * (1.0 - lam_init)).astype(o_ref.dtype)


def diff_attention(qkv, lam_params, subln_g, lam_init, *, tq):
    bsz, seq, c3 = qkv.shape
    n_heads = c3 // (3 * LANES)
    return pl.pallas_call(
        functools.partial(_diff_attn_body, tq, lam_init),
        grid=(bsz, n_heads, seq // tq),
        in_specs=[pl.BlockSpec(lam_params.shape, lambda b, h, i: (0, 0)),
                  pl.BlockSpec((1, LANES), lambda b, h, i: (0, 0)),
                  pl.BlockSpec((None, tq, LANES), lambda b, h, i: (b, i, h)),
                  pl.BlockSpec((None, seq, LANES), lambda b, h, i: (b, 0, n_heads + h)),
                  pl.BlockSpec((None, seq, LANES), lambda b, h, i: (b, 0, 2 * n_heads + h))],
        out_specs=pl.BlockSpec((None, tq, LANES), lambda b, h, i: (b, i, h)),
        out_shape=jax.ShapeDtypeStruct((bsz, seq, n_heads * LANES), BF16),
        name="diff_attention",
        compiler_params=_cparams(("parallel", "parallel", "arbitrary")),
    )(lam_params, subln_g, qkv, qkv, qkv)


def _out_proj_body(ya_ref, yb_ref, wa_ref, wb_ref, h_ref, o_ref):
    acc = jnp.dot(ya_ref[...], wa_ref[...], preferred_element_type=F32)
    acc = acc + jnp.dot(yb_ref[...], wb_ref[...], preferred_element_type=F32)
    o_ref[...] = h_ref[...] + acc


def out_proj(ya, yb, wa, wb, h, *, tm, tn):
    n, d = h.shape
    ca, cb = ya.shape[1], yb.shape[1]
    return pl.pallas_call(
        _out_proj_body,
        grid=(n // tm, d // tn),
        in_specs=[pl.BlockSpec((tm, ca), lambda i, j: (i, 0)),
                  pl.BlockSpec((tm, cb), lambda i, j: (i, 0)),
                  pl.BlockSpec((ca, tn), lambda i, j: (0, j)),
                  pl.BlockSpec((cb, tn), lambda i, j: (0, j)),
                  pl.BlockSpec((tm, tn), lambda i, j: (i, j))],
        out_specs=pl.BlockSpec((tm, tn), lambda i, j: (i, j)),
        out_shape=jax.ShapeDtypeStruct((n, d), F32),
        name="out_proj",
        compiler_params=_cparams(("parallel", "parallel")),
    )(ya, yb, wa, wb, h)


def _swiglu_tile(u, w1_ref, w3_ref, w2_ref):
    a = jnp.dot(u, w1_ref[...], preferred_element_type=F32)
    b = jnp.dot(u, w3_ref[...], preferred_element_type=F32)
    mid = (a * _sigmoid(a) * b).astype(BF16)
    return jnp.dot(mid, w2_ref[...], preferred_element_type=F32)


def _dense_ffn_body(x_ref, g_ref, w1_ref, w3_ref, w2_ref, o_ref, u_s, acc_s):
    f = pl.program_id(1)

    @pl.when(f == 0)
    def _():
        u_s[...] = _rmsnorm_rows(x_ref[...], g_ref[...]).astype(BF16)
        acc_s[...] = jnp.zeros_like(acc_s)

    acc_s[...] += _swiglu_tile(u_s[...], w1_ref, w3_ref, w2_ref)

    @pl.when(f == pl.num_programs(1) - 1)
    def _():
        o_ref[...] = x_ref[...] + acc_s[...]


def dense_ffn(x, g, w1, w3, w2, *, tm, tf):
    n, d = x.shape
    ff = w1.shape[1]
    return pl.pallas_call(
        _dense_ffn_body,
        grid=(n // tm, ff // tf),
        in_specs=[pl.BlockSpec((tm, d), lambda i, f: (i, 0)),
                  pl.BlockSpec((1, d), lambda i, f: (0, 0)),
                  pl.BlockSpec((d, tf), lambda i, f: (0, f)),
                  pl.BlockSpec((d, tf), lambda i, f: (0, f)),
                  pl.BlockSpec((tf, d), lambda i, f: (f, 0))],
        out_specs=pl.BlockSpec((tm, d), lambda i, f: (i, 0)),
        out_shape=jax.ShapeDtypeStruct((n, d), F32),
        scratch_shapes=[pltpu.VMEM((tm, d), BF16), pltpu.VMEM((tm, d), F32)],
        name="dense_ffn",
        compiler_params=_cparams(("parallel", "arbitrary")),
    )(x, g, w1, w3, w2)


def _router_body(n_experts, x_ref, g_ref, wr_ref, u_ref, idx_ref, gate_ref):
    u = _rmsnorm_rows(x_ref[...], g_ref[...])
    u_ref[...] = u.astype(BF16)
    u_hi, u_lo = _split2(u)
    wr = wr_ref[...]
    w_hi, w_lo = _split2(wr)
    logits = (jnp.dot(u_hi, w_hi, preferred_element_type=F32) + jnp.dot(u_hi, w_lo, preferred_element_type=F32)
              + jnp.dot(u_lo, w_hi, preferred_element_type=F32))
    lane = lax.broadcasted_iota(jnp.int32, logits.shape, 1)
    logits = jnp.where(lane < n_experts, logits, NEG_BIG)
    top1 = jnp.max(logits, axis=-1, keepdims=True)
    idx1 = jnp.min(jnp.where(logits == top1, lane, LANES), axis=-1, keepdims=True)
    rest = jnp.where(lane == idx1, NEG_BIG, logits)
    top2 = jnp.max(rest, axis=-1, keepdims=True)
    idx2 = jnp.min(jnp.where(rest == top2, lane, LANES), axis=-1, keepdims=True)
    e2 = jnp.exp(top2 - top1)
    gate1 = 1.0 / (1.0 + e2)
    idx_ref[...] = jnp.where(lane == 0, idx1, jnp.where(lane == 1, idx2, 0))
    gate_ref[...] = jnp.where(lane == 0, gate1, jnp.where(lane == 1, e2 * gate1, 0.0))


def moe_router(x, g, router_w, *, tm):
    n, d = x.shape
    n_experts = router_w.shape[1]
    wr = jnp.zeros((d, LANES), F32).at[:, :n_experts].set(router_w)
    return pl.pallas_call(
        functools.partial(_router_body, n_experts),
        grid=(n // tm,),
        in_specs=[pl.BlockSpec((tm, d), lambda i: (i, 0)),
                  pl.BlockSpec((1, d), lambda i: (0, 0)),
                  pl.BlockSpec((d, LANES), lambda i: (0, 0))],
        out_specs=(pl.BlockSpec((tm, d), lambda i: (i, 0)),
                   pl.BlockSpec((tm, LANES), lambda i: (i, 0)),
                   pl.BlockSpec((tm, LANES), lambda i: (i, 0))),
        out_shape=(jax.ShapeDtypeStruct((n, d), BF16),
                   jax.ShapeDtypeStruct((n, LANES), jnp.int32),
                   jax.ShapeDtypeStruct((n, LANES), F32)),
        name="moe_router",
        compiler_params=_cparams(("parallel",)),
    )(x, g, wr)


def _moe_ffn_body(be_ref, nu_ref, x_ref, gate_ref, w1_ref, w3_ref, w2_ref, o_ref, acc_s):
    i, f = pl.program_id(0), pl.program_id(1)
    used = i < nu_ref[0]

    @pl.when(jnp.logical_and(used, f == 0))
    def _():
        acc_s[...] = jnp.zeros_like(acc_s)

    @pl.when(used)
    def _():
        acc_s[...] += _swiglu_tile(x_ref[...], w1_ref, w3_ref, w2_ref)

    @pl.when(f == pl.num_programs(1) - 1)
    def _():
        o_ref[...] = jnp.where(used, acc_s[...] * gate_ref[...], 0.0)


def moe_ffn(x_rows, row_gate, blk_expert, n_used, w1, w3, w2, *, tm, tf):
    rows, d = x_rows.shape
    ff = w1.shape[2]
    nf = ff // tf

    def w_idx(i, f, be, nu):
        return be[i], jnp.where(i < nu[0], f, nf - 1)

    grid_spec = pltpu.PrefetchScalarGridSpec(
        num_scalar_prefetch=2,
        grid=(rows // tm, nf),
        in_specs=[pl.BlockSpec((tm, d), lambda i, f, be, nu: (i, 0)),
                  pl.BlockSpec((tm, 1), lambda i, f, be, nu: (i, 0)),
                  pl.BlockSpec((None, d, tf), lambda i, f, be, nu: (w_idx(i, f, be, nu)[0], 0, w_idx(i, f, be, nu)[1])),
                  pl.BlockSpec((None, d, tf), lambda i, f, be, nu: (w_idx(i, f, be, nu)[0], 0, w_idx(i, f, be, nu)[1])),
                  pl.BlockSpec((None, tf, d), lambda i, f, be, nu: (w_idx(i, f, be, nu)[0], w_idx(i, f, be, nu)[1], 0))],
        out_specs=pl.BlockSpec((tm, d), lambda i, f, be, nu: (i, 0)),
        scratch_shapes=[pltpu.VMEM((tm, d), F32)],
    )
    return pl.pallas_call(
        _moe_ffn_body, grid_spec=grid_spec,
        out_shape=jax.ShapeDtypeStruct((rows, d), F32),
        name="moe_ffn",
        compiler_params=_cparams(("arbitrary", "arbitrary")),
    )(blk_expert, n_used, x_rows, row_gate, w1, w3, w2)


def moe_layer(h, g, router_w, w1, w3, w2, *, tm_router, tm, tf):
    n, d = h.shape
    n_experts = router_w.shape[1]
    u, idx, gate = moe_router(h, g, router_w, tm=tm_router)
    e_flat = idx[:, :TOP_K].reshape(-1)
    g_flat = gate[:, :TOP_K].reshape(-1)
    m = n * TOP_K
    onehot = (e_flat[:, None] == jnp.arange(n_experts, dtype=jnp.int32)[None, :]).astype(jnp.int32)
    counts = jnp.sum(onehot, axis=0)
    rank = jnp.sum((jnp.cumsum(onehot, axis=0) - onehot) * onehot, axis=1)
    padded = (counts + tm - 1) // tm * tm
    pad_ends = jnp.cumsum(padded)
    dest = (pad_ends - padded)[e_flat] + rank
    n_blocks = m // tm + n_experts
    rows = n_blocks * tm
    row_tok = jnp.zeros((rows,), jnp.int32).at[dest].set(jnp.arange(m, dtype=jnp.int32) // TOP_K)
    row_gate = jnp.zeros((rows,), F32).at[dest].set(g_flat)
    blk_start = jnp.arange(n_blocks, dtype=jnp.int32) * tm
    blk_expert = jnp.minimum(jnp.sum(blk_start[:, None] >= pad_ends[None, :], axis=1), n_experts - 1).astype(jnp.int32)
    n_used = (pad_ends[-1:] // tm).astype(jnp.int32)
    y_rows = moe_ffn(u[row_tok], row_gate[:, None], blk_expert, n_used, w1, w3, w2, tm=tm, tf=tf)
    y_slot = y_rows[dest].reshape(n, TOP_K, d)
    return h + jnp.sum(y_slot, axis=1)


def _ple_body(final_norm, x_ref, g_ref, wg_ref, p_ref, wp_ref, gf_ref, o_ref):
    x = x_ref[...]
    u = _rmsnorm_rows(x, g_ref[...]).astype(BF16)
    gate = _sigmoid(jnp.dot(u, wg_ref[...], preferred_element_type=F32))
    proj = jnp.dot(p_ref[...].astype(BF16), wp_ref[...], preferred_element_type=F32)
    out = x + gate * proj
    if final_norm:
        out = _rmsnorm_rows(out, gf_ref[...])
    o_ref[...] = out


def ple_layer(x, g, w_gate, p, w_proj, g_final, final_norm, *, tm):
    n, d = x.shape
    dp = p.shape[1]
    return pl.pallas_call(
        functools.partial(_ple_body, final_norm),
        grid=(n // tm,),
        in_specs=[pl.BlockSpec((tm, d), lambda i: (i, 0)),
                  pl.BlockSpec((1, d), lambda i: (0, 0)),
                  pl.BlockSpec((d, d), lambda i: (0, 0)),
                  pl.BlockSpec((tm, dp), lambda i: (i, 0)),
                  pl.BlockSpec((dp, d), lambda i: (0, 0)),
                  pl.BlockSpec((1, d), lambda i: (0, 0))],
        out_specs=pl.BlockSpec((tm, d), lambda i: (i, 0)),
        out_shape=jax.ShapeDtypeStruct((n, d), F32),
        name="ple_layer",
        compiler_params=_cparams(("parallel",)),
    )(x, g, w_gate, p, w_proj, g_final)


def _lambda_init(layer_idx):
    return 0.8 - 0.6 * math.exp(-0.3 * layer_idx)


def _tile(n, pref):
    t = min(n, pref)
    while n % t:
        t //= 2
    return t


def kernel(x, p, g_mix, w_in, w_vres_down, mu_shift, mu_vres, rw_w0, rw_w2, rw_a0, rw_a2, rw_g2, rw_v0, rw_v2, rw_k_k, rw_k_a, rw_r_k, rw_lnx_w, rw_lnx_b, diff_lambda, diff_subln, w_out, g_ffn, dense_w1, dense_w3, dense_w2, router_w, moe_w1, moe_w3, moe_w2, g_ple, w_ple_gate, w_ple_proj, g_final):
    bsz, seq, d = x.shape
    depth = w_in.shape[0]
    n = bsz * seq
    c_rwkv = rw_w0.shape[1]
    shift_cols = mu_shift.shape[1]
    zr_cols = 3 * c_rwkv + LORA_COLS
    row = lambda v: v.reshape(1, -1)
    tm = _tile(n, 1024)
    h = x.reshape(n, d)
    v_first = None
    for i in range(depth):
        w_vres = w_vres_down[i - 1] if i > 0 else jnp.zeros((d, D_MV_LORA), F32)
        w_r = jnp.concatenate([w_in[i][:, :shift_cols], w_vres,
                               jnp.zeros((d, zr_cols - shift_cols - D_MV_LORA), F32)], axis=1).astype(BF16)
        w_d = w_in[i][:, shift_cols:].astype(BF16)
        z_r = norm_matmul(h, row(g_mix[i]), w_r, F32, tm=tm, tn=zr_cols // 3)
        qkv = norm_matmul(h, row(g_mix[i]), w_d, BF16, tm=tm, tn=w_d.shape[1] // 3)
        packed = _rwkv_params(i, mu_shift, mu_vres, rw_w0, rw_w2, rw_a0, rw_a2, rw_g2, rw_v0, rw_v2,
                              rw_k_k, rw_k_a, rw_r_k, rw_lnx_w, rw_lnx_b)
        mixed = rwkv_time_mix(z_r.reshape(bsz, seq, zr_cols), *packed, v_first, tb=_tile(seq, 512))
        if i == 0:
            y_rwkv, v_first = mixed
        else:
            y_rwkv = mixed
        y_diff = diff_attention(qkv.reshape(bsz, seq, -1), diff_lambda[i], row(diff_subln[i]), _lambda_init(i),
                                tq=_tile(seq, 512))
        wo = w_out[i].astype(BF16)
        h = out_proj(y_rwkv.reshape(n, c_rwkv), y_diff.reshape(n, -1), wo[:c_rwkv], wo[c_rwkv:], h,
                     tm=tm, tn=_tile(d, 1024))
        if i % 2 == 0:
            j = i // 2
            h = dense_ffn(h, row(g_ffn[i]), dense_w1[j].astype(BF16), dense_w3[j].astype(BF16),
                          dense_w2[j].astype(BF16), tm=_tile(n, 512), tf=512)
        else:
            j = i // 2
            h = moe_layer(h, row(g_ffn[i]), router_w[j], moe_w1[j].astype(BF16), moe_w3[j].astype(BF16),
                          moe_w2[j].astype(BF16), tm_router=_tile(n, 512), tm=_tile(n, 1024), tf=512)
        h = ple_layer(h, row(g_ple[i]), w_ple_gate[i].astype(BF16), p[i].reshape(n, -1),
                      w_ple_proj[i].astype(BF16), row(g_final), i == depth - 1, tm=_tile(n, 512))
    return h.reshape(bsz, seq, d)
```

```python
import functools
import math

import jax
import jax.numpy as jnp
from jax import lax
from jax.experimental import pallas as pl
from jax.experimental.pallas import tpu as pltpu

F32 = jnp.float32
BF16 = jnp.bfloat16

LANES = 128
VMEM_LIMIT = 56 * 1024 * 1024

RWKV_HEAD = 64
HEADS_PER_TILE = LANES // RWKV_HEAD
DIFF_HEAD_DIM = 64
D_DECAY_LORA, D_AAA_LORA, D_MV_LORA, D_GATE_LORA = 64, 64, 32, 160
LORA_COLS = 3 * LANES
TOP_K = 2
RMS_EPS = 1e-6
GN_EPS = 64e-5
SUBLN_EPS = 1e-5

WKV_CHUNK = 64


def _cparams(sem):
    return pltpu.CompilerParams(dimension_semantics=sem, vmem_limit_bytes=VMEM_LIMIT)


def _dot01_right(x, m01, nsplit=3):
    acc, rem = None, x
    for _ in range(nsplit):
        piece = rem.astype(BF16)
        t = jnp.dot(piece, m01, preferred_element_type=F32)
        acc = t if acc is None else acc + t
        rem = rem - piece.astype(F32)
    return acc


def _split2(x):
    hi = x.astype(BF16)
    return hi, (x - hi.astype(F32)).astype(BF16)


def _sdot_tn(a, b):
    return lax.dot_general(a.astype(BF16), b.astype(BF16), (((0,), (0,)), ((), ())), preferred_element_type=F32)


def _bmm(a, b):
    return lax.dot_general(a.astype(BF16), b.astype(BF16), (((2,), (1,)), ((0,), (0,))), preferred_element_type=F32)


def _bmm_nt(a, b):
    return lax.dot_general(a.astype(BF16), b.astype(BF16), (((2,), (2,)), ((0,), (0,))), preferred_element_type=F32)


def _bdot01(m01, x, nsplit):
    acc, rem = None, x
    for _ in range(nsplit):
        piece = rem.astype(BF16)
        t = lax.dot_general(m01, piece, (((2,), (1,)), ((0,), (0,))), preferred_element_type=F32)
        acc = t if acc is None else acc + t
        rem = rem - piece.astype(F32)
    return acc


def _dot3(a, b):
    dn = (((1,), (0,)), ((), ()))
    a_hi, a_lo = _split2(a)
    b_hi, b_lo = _split2(b)
    out = lax.dot_general(a_hi, b_hi, dn, preferred_element_type=F32)
    out = out + lax.dot_general(a_hi, b_lo, dn, preferred_element_type=F32)
    return out + lax.dot_general(a_lo, b_hi, dn, preferred_element_type=F32)


def _token_shift(x, prev_row, mu):
    rolled = pltpu.roll(x, 1, 0)
    row = lax.broadcasted_iota(jnp.int32, x.shape, 0)
    prev = jnp.where(row == 0, prev_row, rolled)
    return x + mu * (prev - x)


def _sigmoid(x):
    return 1.0 / (1.0 + jnp.exp(-x))


def _wkv_body(first_layer, tb, *refs):
    if first_layer:
        (zr, zk, zv, zl, par, mul, w2, a2, g2, v2,
         y_out, vf_out, p_s, prev_s, prevl_s, st_s, y_s, wk_s, mn_s) = refs
        vf_in = None
    else:
        (zr, zk, zv, zl, par, mul, w2, a2, g2, v2, vf_in,
         y_out, p_s, prev_s, prevl_s, st_s, y_s, wk_s, mn_s) = refs
    L = WKV_CHUNK
    n_chunks = tb // L

    @pl.when(pl.program_id(2) == 0)
    def _():
        p_s[...] = jnp.zeros_like(p_s)
        prev_s[...] = jnp.zeros_like(prev_s)
        prevl_s[...] = jnp.zeros_like(prevl_s)

    lane = lax.broadcasted_iota(jnp.int32, (LANES, LANES), 1)
    sub = lax.broadcasted_iota(jnp.int32, (LANES, LANES), 0)
    same_head = (lane // RWKV_HEAD) == (sub // RWKV_HEAD)
    block_ones = jnp.where(same_head, 1.0, 0.0).astype(BF16)

    zr_v, zk_v, zv_v, zl_v = zr[...], zk[...], zv[...], zl[...]
    r = _token_shift(zr_v, prev_s[0:1, :], par[0:1, :])
    k = _token_shift(zk_v, prev_s[1:2, :], par[1:2, :])
    v = _token_shift(zv_v, prev_s[2:3, :], par[2:3, :])
    ls = _token_shift(zl_v, prevl_s[...], mul[...])
    prev_s[0:1, :] = zr_v[tb - 1:tb, :]
    prev_s[1:2, :] = zk_v[tb - 1:tb, :]
    prev_s[2:3, :] = zv_v[tb - 1:tb, :]
    prevl_s[...] = zl_v[tb - 1:tb, :]

    l0 = ls[:, 0:LANES]
    l1 = ls[:, LANES:3 * LANES]
    l2 = ls[:, 2 * LANES:3 * LANES]
    wl = par[3:4, :] + jnp.dot(jnp.tanh(l0).astype(BF16), w2[...], preferred_element_type=F32)
    al = par[4:5, :] + jnp.dot(l0.astype(BF16), a2[...], preferred_element_type=F32)
    gate = jnp.dot(_sigmoid(l1).astype(BF16), g2[...], preferred_element_type=F32)
    if first_layer:
        vf_out[...] = v
    else:
        vg = _sigmoid(par[5:6, :] + jnp.dot(l2.astype(BF16), v2[...], preferred_element_type=F32))
        v = v + (vf_in[...] - v) * vg
    neg_wl = -wl
    softplus = jnp.maximum(neg_wl, 0.0) + jnp.log(1.0 + jnp.exp(-jnp.abs(neg_wl)))
    logw = -jnp.exp(-softplus - 0.5)
    asig = _sigmoid(al)
    kk = k * par[6:7, :]
    kk_norm = jnp.sqrt(_dot01_right(kk * kk, block_ones))
    kk = kk / jnp.maximum(kk_norm, 1e-12)
    k2 = k * (1.0 + (asig - 1.0) * par[7:8, :])
    bonus = _dot01_right(r * k2 * par[8:9, :], block_ones) * v

    st_s[0] = r
    st_s[1] = logw
    st_s[2] = k2
    st_s[3] = v
    st_s[4] = -kk
    st_s[5] = kk * asig
    st_s[6] = gate
    st_s[7] = bonus

    C = n_chunks
    ti = lax.broadcasted_iota(jnp.int32, (C, L, L), 1)
    si = lax.broadcasted_iota(jnp.int32, (C, L, L), 2)
    tri01 = jnp.where(si <= ti, 1.0, 0.0).astype(BF16)
    ti2 = lax.broadcasted_iota(jnp.int32, (2 * C, 2 * L, L), 1)
    si2 = lax.broadcasted_iota(jnp.int32, (2 * C, 2 * L, L), 2)
    g_mask = si2 <= jnp.where(ti2 < L, ti2 - 1, ti2 - L)
    eye = jnp.where(lax.broadcasted_iota(jnp.int32, (2 * C, L, L), 1)
                    == lax.broadcasted_iota(jnp.int32, (2 * C, L, L), 2), 1.0, 0.0).astype(F32)
    head0 = lax.broadcasted_iota(jnp.int32, (C, L, LANES), 2) < RWKV_HEAD

    chunks = lambda x: x.reshape(C, L, LANES)
    r3, lw3, k3, v3, a3, b3 = (chunks(st_s[i]) for i in range(6))
    cum = _bdot01(tri01, lw3, nsplit=2)
    c_last = cum[:, L - 1:L, :]
    e_inv = jnp.exp(-cum)
    e_end = jnp.exp(c_last - cum)
    a_t = a3 * jnp.exp(cum - lw3)
    b_t = b3 * e_inv
    k_t = k3 * e_inv
    r_t = r3 * jnp.exp(cum)
    two = lambda x: jnp.concatenate([x, x], axis=0)
    per_head = lambda x: jnp.concatenate([jnp.where(head0, x, 0.0), jnp.where(head0, 0.0, x)], axis=0)
    lhs = jnp.concatenate([per_head(a_t), per_head(r_t)], axis=1).astype(BF16)
    g_b = jnp.where(g_mask, _bmm_nt(lhs, two(b_t)), 0.0)
    g_k = jnp.where(g_mask, _bmm_nt(lhs, two(k_t)), 0.0)
    a_ab, a_rb = g_b[:, :L], g_b[:, L:]
    a_ak, a_rk = g_k[:, :L], g_k[:, L:]
    t_inv = eye + a_ab
    pw = _bmm(a_ab, a_ab)
    steps = int(math.log2(L)) - 1
    for j in range(steps):
        if j + 1 < steps:
            both = _bmm(jnp.concatenate([t_inv, pw], axis=1), pw)
            t_inv = t_inv + both[:, :L]
            pw = both[:, L:]
        else:
            t_inv = t_inv + _bmm(t_inv, pw)
    v2x = two(v3)
    ta_h = _bmm(t_inv, two(a_t))
    u0_h = _bmm(t_inv, _bmm(a_ak, v2x))
    rp_h = _bmm(a_rb, ta_h)
    y1_h = _bmm(a_rb, u0_h) + _bmm(a_rk, v2x)
    merge = lambda x: jnp.where(head0, x[:C], x[C:])
    wk_s[0] = merge(ta_h).reshape(tb, LANES)
    wk_s[1] = merge(u0_h).reshape(tb, LANES)
    wk_s[2] = (r_t + merge(rp_h)).reshape(tb, LANES)
    wk_s[3] = merge(y1_h).reshape(tb, LANES)
    wk_s[4] = (b3 * e_end).reshape(tb, LANES)
    wk_s[5] = (k3 * e_end).reshape(tb, LANES)
    wk_s[6] = cum.reshape(tb, LANES)

    diag = lane == sub

    def chunk_maps(c, carry):
        sl = pl.ds(pl.multiple_of(c * L, L), L)
        ta_c, u0_c, bh_c, kh_c = wk_s[0, sl, :], wk_s[1, sl, :], wk_s[4, sl, :], wk_s[5, sl, :]
        tail = wk_s[6, pl.ds(pl.multiple_of(c * L + L - 8, 8), 8), :]
        w_end = jnp.exp(tail[7:8, :])
        m = _sdot_tn(bh_c, ta_c) + jnp.where(diag, w_end, 0.0)
        n = _sdot_tn(bh_c, u0_c) + _sdot_tn(kh_c, st_s[3, sl, :])
        mn_s[c, :, 0:LANES] = jnp.where(same_head, m, 0.0)
        mn_s[c, :, LANES:2 * LANES] = jnp.where(same_head, n, 0.0)
        return carry

    lax.fori_loop(0, C, chunk_maps, 0, unroll=2)

    def chain(c, p):
        sl = pl.ds(pl.multiple_of(c * L, L), L)
        y_s[sl, :] = _dot3(wk_s[2, sl, :], p) + wk_s[3, sl, :]
        return _dot3(mn_s[c, :, 0:LANES], p) + mn_s[c, :, LANES:2 * LANES]

    p_s[...] = lax.fori_loop(0, C, chain, p_s[...])

    y = y_s[...]
    inv_n = 1.0 / RWKV_HEAD
    mean = _dot01_right(y, block_ones) * inv_n
    d = y - mean
    var = _dot01_right(d * d, block_ones) * inv_n
    yn = d * lax.rsqrt(var + GN_EPS) * par[9:10, :] + par[10:11, :]
    y_out[...] = ((yn + st_s[7]) * st_s[6]).astype(y_out.dtype)


def rwkv_time_mix(z_r, par, mul, w2p, a2p, g2p, v2p, v_first, *, tb):
    bsz, seq, cols = z_r.shape
    c_rwkv = (cols - LORA_COLS) // 3
    n_tiles = c_rwkv // LANES
    first = v_first is None
    grid = (bsz, n_tiles, seq // tb)
    col_spec = lambda off: pl.BlockSpec((None, tb, LANES), lambda b, h, t: (b, t, off + h))
    par_rows = par.shape[0]
    in_specs = [
        col_spec(0), col_spec(n_tiles), col_spec(2 * n_tiles),
        pl.BlockSpec((None, tb, LORA_COLS), lambda b, h, t: (b, t, 3 * c_rwkv // LORA_COLS)),
        pl.BlockSpec((par_rows, LANES), lambda b, h, t: (0, h)),
        pl.BlockSpec((1, LORA_COLS), lambda b, h, t: (0, 0)),
        pl.BlockSpec((LANES, LANES), lambda b, h, t: (0, h)),
        pl.BlockSpec((LANES, LANES), lambda b, h, t: (0, h)),
        pl.BlockSpec((2 * LANES, LANES), lambda b, h, t: (0, h)),
        pl.BlockSpec((LANES, LANES), lambda b, h, t: (0, h)),
    ]
    args = [z_r, z_r, z_r, z_r, par, mul, w2p, a2p, g2p, v2p]
    y_spec = pl.BlockSpec((None, tb, LANES), lambda b, h, t: (b, t, h))
    y_shape = jax.ShapeDtypeStruct((bsz, seq, c_rwkv), BF16)
    if first:
        out_shape = (y_shape, jax.ShapeDtypeStruct((bsz, seq, c_rwkv), F32))
        out_specs = (y_spec, y_spec)
    else:
        in_specs.append(y_spec)
        args.append(v_first)
        out_shape = y_shape
        out_specs = y_spec
    scratch = [
        pltpu.VMEM((LANES, LANES), F32),
        pltpu.VMEM((8, LANES), F32),
        pltpu.VMEM((1, LORA_COLS), F32),
        pltpu.VMEM((8, tb, LANES), F32),
        pltpu.VMEM((tb, LANES), F32),
        pltpu.VMEM((7, tb, LANES), F32),
        pltpu.VMEM((tb // WKV_CHUNK, LANES, 2 * LANES), F32),
    ]
    return pl.pallas_call(
        functools.partial(_wkv_body, first, tb),
        grid=grid, in_specs=in_specs, out_specs=out_specs, out_shape=out_shape,
        scratch_shapes=scratch, name="rwkv7_time_mix",
        compiler_params=_cparams(("parallel", "parallel", "arbitrary")),
    )(*args)


def _rwkv_params(i, mu_shift, mu_vres, rw_w0, rw_w2, rw_a0, rw_a2, rw_g2, rw_v0, rw_v2,
                 rw_k_k, rw_k_a, rw_r_k, rw_lnx_w, rw_lnx_b):
    c = rw_w0.shape[1]
    mu = mu_shift[i]
    zero_c = jnp.zeros((c,), F32)
    rows = [mu[:c], mu[c:2 * c], mu[2 * c:3 * c], rw_w0[i], rw_a0[i],
            rw_v0[i - 1] if i > 0 else zero_c, rw_k_k[i], rw_k_a[i], rw_r_k[i].reshape(-1),
            rw_lnx_w[i], rw_lnx_b[i]]
    par = jnp.stack(rows + [zero_c] * (16 - len(rows)))
    n_lora = D_DECAY_LORA + D_AAA_LORA + D_GATE_LORA
    mu_v = mu_vres[i - 1] if i > 0 else jnp.zeros((D_MV_LORA,), F32)
    mul = jnp.concatenate([mu[3 * c:], mu_v, jnp.zeros((LORA_COLS - n_lora - D_MV_LORA,), F32)])[None, :]
    w2p = jnp.zeros((LANES, c), F32).at[:D_DECAY_LORA].set(rw_w2[i])
    a2p = jnp.zeros((LANES, c), F32).at[D_DECAY_LORA:D_DECAY_LORA + D_AAA_LORA].set(rw_a2[i])
    g2p = jnp.zeros((2 * LANES, c), F32).at[:D_GATE_LORA].set(rw_g2[i])
    v2p = jnp.zeros((LANES, c), F32)
    if i > 0:
        lv0 = n_lora - 2 * LANES
        v2p = v2p.at[lv0:lv0 + D_MV_LORA].set(rw_v2[i - 1])
    return par, mul, w2p.astype(BF16), a2p.astype(BF16), g2p.astype(BF16), v2p.astype(BF16)


def _rmsnorm_rows(x, g):
    ms = jnp.mean(x * x, axis=-1, keepdims=True)
    return x * lax.rsqrt(ms + RMS_EPS) * g


def _norm_matmul_body(x_ref, g_ref, w_ref, o_ref, u_s):
    @pl.when(pl.program_id(1) == 0)
    def _():
        u_s[...] = _rmsnorm_rows(x_ref[...], g_ref[...]).astype(BF16)

    o_ref[...] = jnp.dot(u_s[...], w_ref[...], preferred_element_type=F32).astype(o_ref.dtype)


def norm_matmul(x, g, w, out_dtype, *, tm, tn):
    n, d = x.shape
    c = w.shape[1]
    return pl.pallas_call(
        _norm_matmul_body,
        grid=(n // tm, c // tn),
        in_specs=[pl.BlockSpec((tm, d), lambda i, j: (i, 0)),
                  pl.BlockSpec((1, d), lambda i, j: (0, 0)),
                  pl.BlockSpec((d, tn), lambda i, j: (0, j))],
        out_specs=pl.BlockSpec((tm, tn), lambda i, j: (i, j)),
        out_shape=jax.ShapeDtypeStruct((n, c), out_dtype),
        scratch_shapes=[pltpu.VMEM((tm, d), BF16)],
        name="norm_matmul",
        compiler_params=_cparams(("parallel", "arbitrary")),
    )(x, g, w)


NEG_BIG = -1e30


def _diff_attn_body(tq, lam_init, lp_ref, g_ref, q_ref, kt_ref, v_ref, o_ref):
    qi = pl.program_id(2)
    d = DIFF_HEAD_DIM
    lp = lp_ref[...]
    lam = (jnp.exp(jnp.sum(lp[0:1] * lp[1:2], axis=-1, keepdims=True))
           - jnp.exp(jnp.sum(lp[2:3] * lp[3:4], axis=-1, keepdims=True)) + lam_init)
    q = q_ref[...] * jnp.asarray(d ** -0.5, BF16)
    lane = lax.broadcasted_iota(jnp.int32, q.shape, 1)
    q_maps = (jnp.where(lane < d, q, jnp.zeros_like(q)), jnp.where(lane >= d, q, jnp.zeros_like(q)))

    def step(j, carry, masked):
        start = pl.multiple_of(j * tq, tq)
        kt = kt_ref[:, pl.ds(start, tq)]
        vb = v_ref[pl.ds(start, tq), :]
        new = []
        for c in range(2):
            m, l, acc = carry[c]
            s = jnp.dot(q_maps[c], kt, preferred_element_type=F32)
            if masked:
                row = lax.broadcasted_iota(jnp.int32, s.shape, 0)
                col = lax.broadcasted_iota(jnp.int32, s.shape, 1)
                s = jnp.where(col <= row, s, NEG_BIG)
            m_new = jnp.maximum(m, jnp.max(s, axis=-1, keepdims=True))
            alpha = jnp.exp(m - m_new)
            pr = jnp.exp(s - m_new)
            l_new = alpha * l + jnp.sum(pr, axis=-1, keepdims=True)
            acc_new = alpha * acc + jnp.dot(pr.astype(BF16), vb, preferred_element_type=F32)
            new.append((m_new, l_new, acc_new))
        return tuple(new)

    init_one = (jnp.full((tq, 1), NEG_BIG, F32), jnp.zeros((tq, 1), F32), jnp.zeros((tq, LANES), F32))
    carry = lax.fori_loop(0, qi, lambda j, c: step(j, c, False), (init_one, init_one))
    (_, l1, acc1), (_, l2, acc2) = step(qi, carry, True)
    o = acc1 / l1 - lam * (acc2 / l2)
    o = o * lax.rsqrt(jnp.mean(o * o, axis=-1, keepdims=True) + SUBLN_EPS)
    o_ref[...] = (o * g_ref[...] * (1.0 - lam_init)).astype(o_ref.dtype)


def diff_attention(qkv, kt, lam_params, subln_g, lam_init, *, tq):
    bsz, seq, c3 = qkv.shape
    n_heads = c3 // (3 * LANES)
    return pl.pallas_call(
        functools.partial(_diff_attn_body, tq, lam_init),
        grid=(bsz, n_heads, seq // tq),
        in_specs=[pl.BlockSpec(lam_params.shape, lambda b, h, i: (0, 0)),
                  pl.BlockSpec((1, LANES), lambda b, h, i: (0, 0)),
                  pl.BlockSpec((None, tq, LANES), lambda b, h, i: (b, i, h)),
                  pl.BlockSpec((None, None, LANES, seq), lambda b, h, i: (b, h, 0, 0)),
                  pl.BlockSpec((None, seq, LANES), lambda b, h, i: (b, 0, 2 * n_heads + h))],
        out_specs=pl.BlockSpec((None, tq, LANES), lambda b, h, i: (b, i, h)),
        out_shape=jax.ShapeDtypeStruct((bsz, seq, n_heads * LANES), BF16),
        name="diff_attention",
        compiler_params=_cparams(("parallel", "parallel", "arbitrary")),
    )(lam_params, subln_g, qkv, kt, qkv)


def _out_proj_body(ya_ref, yb_ref, wa_ref, wb_ref, h_ref, o_ref):
    acc = jnp.dot(ya_ref[...], wa_ref[...], preferred_element_type=F32)
    acc = acc + jnp.dot(yb_ref[...], wb_ref[...], preferred_element_type=F32)
    o_ref[...] = h_ref[...] + acc


def out_proj(ya, yb, wa, wb, h, *, tm, tn):
    n, d = h.shape
    ca, cb = ya.shape[1], yb.shape[1]
    return pl.pallas_call(
        _out_proj_body,
        grid=(n // tm, d // tn),
        in_specs=[pl.BlockSpec((tm, ca), lambda i, j: (i, 0)),
                  pl.BlockSpec((tm, cb), lambda i, j: (i, 0)),
                  pl.BlockSpec((ca, tn), lambda i, j: (0, j)),
                  pl.BlockSpec((cb, tn), lambda i, j: (0, j)),
                  pl.BlockSpec((tm, tn), lambda i, j: (i, j))],
        out_specs=pl.BlockSpec((tm, tn), lambda i, j: (i, j)),
        out_shape=jax.ShapeDtypeStruct((n, d), F32),
        name="out_proj",
        compiler_params=_cparams(("parallel", "parallel")),
    )(ya, yb, wa, wb, h)


def _swiglu_tile(u, w1_ref, w3_ref, w2_ref):
    a = jnp.dot(u, w1_ref[...], preferred_element_type=F32)
    b = jnp.dot(u, w3_ref[...], preferred_element_type=F32)
    mid = (a * _sigmoid(a) * b).astype(BF16)
    return jnp.dot(mid, w2_ref[...], preferred_element_type=F32)


def _dense_ffn_body(x_ref, g_ref, w1_ref, w3_ref, w2_ref, o_ref, u_s, acc_s):
    f = pl.program_id(1)

    @pl.when(f == 0)
    def _():
        u_s[...] = _rmsnorm_rows(x_ref[...], g_ref[...]).astype(BF16)
        acc_s[...] = jnp.zeros_like(acc_s)

    acc_s[...] += _swiglu_tile(u_s[...], w1_ref, w3_ref, w2_ref)

    @pl.when(f == pl.num_programs(1) - 1)
    def _():
        o_ref[...] = x_ref[...] + acc_s[...]


def dense_ffn(x, g, w1, w3, w2, *, tm, tf):
    n, d = x.shape
    ff = w1.shape[1]
    return pl.pallas_call(
        _dense_ffn_body,
        grid=(n // tm, ff // tf),
        in_specs=[pl.BlockSpec((tm, d), lambda i, f: (i, 0)),
                  pl.BlockSpec((1, d), lambda i, f: (0, 0)),
                  pl.BlockSpec((d, tf), lambda i, f: (0, f)),
                  pl.BlockSpec((d, tf), lambda i, f: (0, f)),
                  pl.BlockSpec((tf, d), lambda i, f: (f, 0))],
        out_specs=pl.BlockSpec((tm, d), lambda i, f: (i, 0)),
        out_shape=jax.ShapeDtypeStruct((n, d), F32),
        scratch_shapes=[pltpu.VMEM((tm, d), BF16), pltpu.VMEM((tm, d), F32)],
        name="dense_ffn",
        compiler_params=_cparams(("parallel", "arbitrary")),
    )(x, g, w1, w3, w2)


def _router_body(n_experts, x_ref, g_ref, wr_ref, u_ref, idx_ref, gate_ref):
    u = _rmsnorm_rows(x_ref[...], g_ref[...])
    u_ref[...] = u.astype(BF16)
    u_hi, u_lo = _split2(u)
    wr = wr_ref[...]
    w_hi, w_lo = _split2(wr)
    logits = (jnp.dot(u_hi, w_hi, preferred_element_type=F32) + jnp.dot(u_hi, w_lo, preferred_element_type=F32)
              + jnp.dot(u_lo, w_hi, preferred_element_type=F32))
    lane = lax.broadcasted_iota(jnp.int32, logits.shape, 1)
    logits = jnp.where(lane < n_experts, logits, NEG_BIG)
    top1 = jnp.max(logits, axis=-1, keepdims=True)
    idx1 = jnp.min(jnp.where(logits == top1, lane, LANES), axis=-1, keepdims=True)
    rest = jnp.where(lane == idx1, NEG_BIG, logits)
    top2 = jnp.max(rest, axis=-1, keepdims=True)
    idx2 = jnp.min(jnp.where(rest == top2, lane, LANES), axis=-1, keepdims=True)
    e2 = jnp.exp(top2 - top1)
    gate1 = 1.0 / (1.0 + e2)
    idx_ref[...] = jnp.where(lane == 0, idx1, jnp.where(lane == 1, idx2, 0))
    gate_ref[...] = jnp.where(lane == 0, gate1, jnp.where(lane == 1, e2 * gate1, 0.0))


def moe_router(x, g, router_w, *, tm):
    n, d = x.shape
    n_experts = router_w.shape[1]
    wr = jnp.zeros((d, LANES), F32).at[:, :n_experts].set(router_w)
    return pl.pallas_call(
        functools.partial(_router_body, n_experts),
        grid=(n // tm,),
        in_specs=[pl.BlockSpec((tm, d), lambda i: (i, 0)),
                  pl.BlockSpec((1, d), lambda i: (0, 0)),
                  pl.BlockSpec((d, LANES), lambda i: (0, 0))],
        out_specs=(pl.BlockSpec((tm, d), lambda i: (i, 0)),
                   pl.BlockSpec((tm, LANES), lambda i: (i, 0)),
                   pl.BlockSpec((tm, LANES), lambda i: (i, 0))),
        out_shape=(jax.ShapeDtypeStruct((n, d), BF16),
                   jax.ShapeDtypeStruct((n, LANES), jnp.int32),
                   jax.ShapeDtypeStruct((n, LANES), F32)),
        name="moe_router",
        compiler_params=_cparams(("parallel",)),
    )(x, g, wr)


def _moe_ffn_body(be_ref, nu_ref, x_ref, gate_ref, w1_ref, w3_ref, w2_ref, o_ref, acc_s):
    i, f = pl.program_id(0), pl.program_id(1)
    used = i < nu_ref[0]

    @pl.when(jnp.logical_and(used, f == 0))
    def _():
        acc_s[...] = jnp.zeros_like(acc_s)

    @pl.when(used)
    def _():
        acc_s[...] += _swiglu_tile(x_ref[...], w1_ref, w3_ref, w2_ref)

    @pl.when(f == pl.num_programs(1) - 1)
    def _():
        o_ref[...] = jnp.where(used, acc_s[...] * gate_ref[...], 0.0)


def moe_ffn(x_rows, row_gate, blk_expert, n_used, w1, w3, w2, *, tm, tf):
    rows, d = x_rows.shape
    ff = w1.shape[2]
    nf = ff // tf

    def w_idx(i, f, be, nu):
        return be[i], jnp.where(i < nu[0], f, nf - 1)

    grid_spec = pltpu.PrefetchScalarGridSpec(
        num_scalar_prefetch=2,
        grid=(rows // tm, nf),
        in_specs=[pl.BlockSpec((tm, d), lambda i, f, be, nu: (i, 0)),
                  pl.BlockSpec((tm, 1), lambda i, f, be, nu: (i, 0)),
                  pl.BlockSpec((None, d, tf), lambda i, f, be, nu: (w_idx(i, f, be, nu)[0], 0, w_idx(i, f, be, nu)[1])),
                  pl.BlockSpec((None, d, tf), lambda i, f, be, nu: (w_idx(i, f, be, nu)[0], 0, w_idx(i, f, be, nu)[1])),
                  pl.BlockSpec((None, tf, d), lambda i, f, be, nu: (w_idx(i, f, be, nu)[0], w_idx(i, f, be, nu)[1], 0))],
        out_specs=pl.BlockSpec((tm, d), lambda i, f, be, nu: (i, 0)),
        scratch_shapes=[pltpu.VMEM((tm, d), F32)],
    )
    return pl.pallas_call(
        _moe_ffn_body, grid_spec=grid_spec,
        out_shape=jax.ShapeDtypeStruct((rows, d), F32),
        name="moe_ffn",
        compiler_params=_cparams(("arbitrary", "arbitrary")),
    )(blk_expert, n_used, x_rows, row_gate, w1, w3, w2)


def moe_layer(h, g, router_w, w1, w3, w2, *, tm_router, tm, tf):
    n, d = h.shape
    n_experts = router_w.shape[1]
    u, idx, gate = moe_router(h, g, router_w, tm=tm_router)
    e_flat = idx[:, :TOP_K].reshape(-1)
    g_flat = gate[:, :TOP_K].reshape(-1)
    m = n * TOP_K
    onehot = (e_flat[:, None] == jnp.arange(n_experts, dtype=jnp.int32)[None, :]).astype(jnp.int32)
    counts = jnp.sum(onehot, axis=0)
    rank = jnp.sum((jnp.cumsum(onehot, axis=0) - onehot) * onehot, axis=1)
    padded = (counts + tm - 1) // tm * tm
    pad_ends = jnp.cumsum(padded)
    dest = (pad_ends - padded)[e_flat] + rank
    n_blocks = m // tm + n_experts
    rows = n_blocks * tm
    row_tok = jnp.zeros((rows,), jnp.int32).at[dest].set(jnp.arange(m, dtype=jnp.int32) // TOP_K)
    row_gate = jnp.zeros((rows,), F32).at[dest].set(g_flat)
    blk_start = jnp.arange(n_blocks, dtype=jnp.int32) * tm
    blk_expert = jnp.minimum(jnp.sum(blk_start[:, None] >= pad_ends[None, :], axis=1), n_experts - 1).astype(jnp.int32)
    n_used = (pad_ends[-1:] // tm).astype(jnp.int32)
    y_rows = moe_ffn(u[row_tok], row_gate[:, None], blk_expert, n_used, w1, w3, w2, tm=tm, tf=tf)
    y_slot = y_rows[dest].reshape(n, TOP_K, d)
    return h + jnp.sum(y_slot, axis=1)


def _ple_body(final_norm, x_ref, g_ref, wg_ref, p_ref, wp_ref, gf_ref, o_ref):
    x = x_ref[...]
    u = _rmsnorm_rows(x, g_ref[...]).astype(BF16)
    gate = _sigmoid(jnp.dot(u, wg_ref[...], preferred_element_type=F32))
    proj = jnp.dot(p_ref[...].astype(BF16), wp_ref[...], preferred_element_type=F32)
    out = x + gate * proj
    if final_norm:
        out = _rmsnorm_rows(out, gf_ref[...])
    o_ref[...] = out


def ple_layer(x, g, w_gate, p, w_proj, g_final, final_norm, *, tm):
    n, d = x.shape
    dp = p.shape[1]
    return pl.pallas_call(
        functools.partial(_ple_body, final_norm),
        grid=(n // tm,),
        in_specs=[pl.BlockSpec((tm, d), lambda i: (i, 0)),
                  pl.BlockSpec((1, d), lambda i: (0, 0)),
                  pl.BlockSpec((d, d), lambda i: (0, 0)),
                  pl.BlockSpec((tm, dp), lambda i: (i, 0)),
                  pl.BlockSpec((dp, d), lambda i: (0, 0)),
                  pl.BlockSpec((1, d), lambda i: (0, 0))],
        out_specs=pl.BlockSpec((tm, d), lambda i: (i, 0)),
        out_shape=jax.ShapeDtypeStruct((n, d), F32),
        name="ple_layer",
        compiler_params=_cparams(("parallel",)),
    )(x, g, w_gate, p, w_proj, g_final)


def _lambda_init(layer_idx):
    return 0.8 - 0.6 * math.exp(-0.3 * layer_idx)


def _tile(n, pref):
    t = min(n, pref)
    while n % t:
        t //= 2
    return t


def kernel(x, p, g_mix, w_in, w_vres_down, mu_shift, mu_vres, rw_w0, rw_w2, rw_a0, rw_a2, rw_g2, rw_v0, rw_v2, rw_k_k, rw_k_a, rw_r_k, rw_lnx_w, rw_lnx_b, diff_lambda, diff_subln, w_out, g_ffn, dense_w1, dense_w3, dense_w2, router_w, moe_w1, moe_w3, moe_w2, g_ple, w_ple_gate, w_ple_proj, g_final):
    bsz, seq, d = x.shape
    depth = w_in.shape[0]
    n = bsz * seq
    c_rwkv = rw_w0.shape[1]
    shift_cols = mu_shift.shape[1]
    zr_cols = 3 * c_rwkv + LORA_COLS
    row = lambda v: v.reshape(1, -1)
    tm = _tile(n, 1024)
    h = x.reshape(n, d)
    v_first = None
    for i in range(depth):
        w_vres = w_vres_down[i - 1] if i > 0 else jnp.zeros((d, D_MV_LORA), F32)
        w_r = jnp.concatenate([w_in[i][:, :shift_cols], w_vres,
                               jnp.zeros((d, zr_cols - shift_cols - D_MV_LORA), F32)], axis=1).astype(BF16)
        w_d = w_in[i][:, shift_cols:].astype(BF16)
        z_r = norm_matmul(h, row(g_mix[i]), w_r, F32, tm=tm, tn=zr_cols // 3)
        qkv = norm_matmul(h, row(g_mix[i]), w_d, BF16, tm=tm, tn=w_d.shape[1] // 3)
        packed = _rwkv_params(i, mu_shift, mu_vres, rw_w0, rw_w2, rw_a0, rw_a2, rw_g2, rw_v0, rw_v2,
                              rw_k_k, rw_k_a, rw_r_k, rw_lnx_w, rw_lnx_b)
        mixed = rwkv_time_mix(z_r.reshape(bsz, seq, zr_cols), *packed, v_first, tb=_tile(seq, 512))
        if i == 0:
            y_rwkv, v_first = mixed
        else:
            y_rwkv = mixed
        qkv = qkv.reshape(bsz, seq, -1)
        c_diff = qkv.shape[-1] // 3
        kt = qkv[:, :, c_diff:2 * c_diff].reshape(bsz, seq, c_diff // LANES, LANES).transpose(0, 2, 3, 1)
        y_diff = diff_attention(qkv, kt, diff_lambda[i], row(diff_subln[i]), _lambda_init(i), tq=_tile(seq, 1024))
        wo = w_out[i].astype(BF16)
        h = out_proj(y_rwkv.reshape(n, c_rwkv), y_diff.reshape(n, -1), wo[:c_rwkv], wo[c_rwkv:], h,
                     tm=tm, tn=_tile(d, 1024))
        if i % 2 == 0:
            j = i // 2
            h = dense_ffn(h, row(g_ffn[i]), dense_w1[j].astype(BF16), dense_w3[j].astype(BF16),
                          dense_w2[j].astype(BF16), tm=_tile(n, 512), tf=512)
        else:
            j = i // 2
            h = moe_layer(h, row(g_ffn[i]), router_w[j], moe_w1[j].astype(BF16), moe_w3[j].astype(BF16),
                          moe_w2[j].astype(BF16), tm_router=_tile(n, 512), tm=_tile(n, 1024), tf=512)
        h = ple_layer(h, row(g_ple[i]), w_ple_gate[i].astype(BF16), p[i].reshape(n, -1),
                      w_ple_proj[i].astype(BF16), row(g_final), i == depth - 1, tm=_tile(n, 512))
    return h.reshape(bsz, seq, d)
```

```python
import functools
import math

import jax
import jax.numpy as jnp
from jax import lax
from jax.experimental import pallas as pl
from jax.experimental.pallas import tpu as pltpu

F32 = jnp.float32
BF16 = jnp.bfloat16

LANES = 128
VMEM_LIMIT = 56 * 1024 * 1024

RWKV_HEAD = 64
HEADS_PER_TILE = LANES // RWKV_HEAD
DIFF_HEAD_DIM = 64
D_DECAY_LORA, D_AAA_LORA, D_MV_LORA, D_GATE_LORA = 64, 64, 32, 160
LORA_COLS = 3 * LANES
TOP_K = 2
RMS_EPS = 1e-6
GN_EPS = 64e-5
SUBLN_EPS = 1e-5

WKV_CHUNK = 64


def _cparams(sem):
    return pltpu.CompilerParams(dimension_semantics=sem, vmem_limit_bytes=VMEM_LIMIT)


def _dot01_right(x, m01, nsplit=3):
    acc, rem = None, x
    for _ in range(nsplit):
        piece = rem.astype(BF16)
        t = jnp.dot(piece, m01, preferred_element_type=F32)
        acc = t if acc is None else acc + t
        rem = rem - piece.astype(F32)
    return acc


def _split2(x):
    hi = x.astype(BF16)
    return hi, (x - hi.astype(F32)).astype(BF16)


def _sdot_tn(a, b):
    return lax.dot_general(a.astype(BF16), b.astype(BF16), (((0,), (0,)), ((), ())), preferred_element_type=F32)


def _bmm(a, b):
    return lax.dot_general(a.astype(BF16), b.astype(BF16), (((2,), (1,)), ((0,), (0,))), preferred_element_type=F32)


def _bmm_nt(a, b):
    return lax.dot_general(a.astype(BF16), b.astype(BF16), (((2,), (2,)), ((0,), (0,))), preferred_element_type=F32)


def _bdot01(m01, x, nsplit):
    acc, rem = None, x
    for _ in range(nsplit):
        piece = rem.astype(BF16)
        t = lax.dot_general(m01, piece, (((2,), (1,)), ((0,), (0,))), preferred_element_type=F32)
        acc = t if acc is None else acc + t
        rem = rem - piece.astype(F32)
    return acc


def _dot3(a, b):
    dn = (((1,), (0,)), ((), ()))
    a_hi, a_lo = _split2(a)
    b_hi, b_lo = _split2(b)
    out = lax.dot_general(a_hi, b_hi, dn, preferred_element_type=F32)
    out = out + lax.dot_general(a_hi, b_lo, dn, preferred_element_type=F32)
    return out + lax.dot_general(a_lo, b_hi, dn, preferred_element_type=F32)


def _token_shift(x, prev_row, mu):
    rolled = pltpu.roll(x, 1, 0)
    row = lax.broadcasted_iota(jnp.int32, x.shape, 0)
    prev = jnp.where(row == 0, prev_row, rolled)
    return x + mu * (prev - x)


def _sigmoid(x):
    return 1.0 / (1.0 + jnp.exp(-x))


def _wkv_body(first_layer, tb, nt, *refs):
    if first_layer:
        (zr, zk, zv, zl, par, mul, w2, a2, g2, v2,
         y_out, vf_out, p_s, prev_s, prevl_s, st_s, y_s, wk_s, mn_s) = refs
        vf_in = None
    else:
        (zr, zk, zv, zl, par, mul, w2, a2, g2, v2, vf_in,
         y_out, p_s, prev_s, prevl_s, st_s, y_s, wk_s, mn_s) = refs
    L = WKV_CHUNK
    n_chunks = tb // L

    @pl.when(pl.program_id(2) == 0)
    def _():
        p_s[...] = jnp.zeros_like(p_s)
        prev_s[...] = jnp.zeros_like(prev_s)
        prevl_s[...] = jnp.zeros_like(prevl_s)

    lane = lax.broadcasted_iota(jnp.int32, (LANES, LANES), 1)
    sub = lax.broadcasted_iota(jnp.int32, (LANES, LANES), 0)
    same_head = (lane // RWKV_HEAD) == (sub // RWKV_HEAD)
    block_ones = jnp.where(same_head, 1.0, 0.0).astype(BF16)

    zl_v = zl[...]
    ls = _token_shift(zl_v, prevl_s[...], mul[...])
    prevl_s[...] = zl_v[tb - 1:tb, :]
    l0 = ls[:, 0:LANES]
    l0_tanh = jnp.tanh(l0).astype(BF16)
    l0 = l0.astype(BF16)
    l1_sig = _sigmoid(ls[:, LANES:3 * LANES]).astype(BF16)
    l2 = ls[:, 2 * LANES:3 * LANES].astype(BF16)
    for u in range(nt):
        cols = slice(u * LANES, (u + 1) * LANES)
        rows = slice(u * tb, (u + 1) * tb)
        zr_v, zk_v, zv_v = zr[:, cols], zk[:, cols], zv[:, cols]
        r = _token_shift(zr_v, prev_s[u, 0:1, :], par[0:1, cols])
        k = _token_shift(zk_v, prev_s[u, 1:2, :], par[1:2, cols])
        v = _token_shift(zv_v, prev_s[u, 2:3, :], par[2:3, cols])
        prev_s[u, 0:1, :] = zr_v[tb - 1:tb, :]
        prev_s[u, 1:2, :] = zk_v[tb - 1:tb, :]
        prev_s[u, 2:3, :] = zv_v[tb - 1:tb, :]
        wl = par[3:4, cols] + jnp.dot(l0_tanh, w2[:, cols], preferred_element_type=F32)
        al = par[4:5, cols] + jnp.dot(l0, a2[:, cols], preferred_element_type=F32)
        gate = jnp.dot(l1_sig, g2[:, cols], preferred_element_type=F32)
        if first_layer:
            vf_out[:, cols] = v
        else:
            vg = _sigmoid(par[5:6, cols] + jnp.dot(l2, v2[:, cols], preferred_element_type=F32))
            v = v + (vf_in[:, cols] - v) * vg
        neg_wl = -wl
        softplus = jnp.maximum(neg_wl, 0.0) + jnp.log(1.0 + jnp.exp(-jnp.abs(neg_wl)))
        logw = -jnp.exp(-softplus - 0.5)
        asig = _sigmoid(al)
        kk = k * par[6:7, cols]
        kk_norm = jnp.sqrt(_dot01_right(kk * kk, block_ones))
        kk = kk / jnp.maximum(kk_norm, 1e-12)
        k2 = k * (1.0 + (asig - 1.0) * par[7:8, cols])
        bonus = _dot01_right(r * k2 * par[8:9, cols], block_ones) * v
        st_s[0, rows, :] = r
        st_s[1, rows, :] = logw
        st_s[2, rows, :] = k2
        st_s[3, rows, :] = v
        st_s[4, rows, :] = -kk
        st_s[5, rows, :] = kk * asig
        st_s[6, rows, :] = gate
        st_s[7, rows, :] = bonus

    C = nt * n_chunks
    tb_all = nt * tb
    ti = lax.broadcasted_iota(jnp.int32, (C, L, L), 1)
    si = lax.broadcasted_iota(jnp.int32, (C, L, L), 2)
    tri01 = jnp.where(si <= ti, 1.0, 0.0).astype(BF16)
    ti2 = lax.broadcasted_iota(jnp.int32, (2 * C, 2 * L, L), 1)
    si2 = lax.broadcasted_iota(jnp.int32, (2 * C, 2 * L, L), 2)
    g_mask = si2 <= jnp.where(ti2 < L, ti2 - 1, ti2 - L)
    eye = jnp.where(lax.broadcasted_iota(jnp.int32, (2 * C, L, L), 1)
                    == lax.broadcasted_iota(jnp.int32, (2 * C, L, L), 2), 1.0, 0.0).astype(F32)
    head0 = lax.broadcasted_iota(jnp.int32, (C, L, LANES), 2) < RWKV_HEAD

    chunks = lambda x: x.reshape(C, L, LANES)
    r3, lw3, k3, v3, a3, b3 = (chunks(st_s[i]) for i in range(6))
    cum = _bdot01(tri01, lw3, nsplit=2)
    c_last = cum[:, L - 1:L, :]
    e_inv = jnp.exp(-cum)
    e_end = jnp.exp(c_last - cum)
    a_t = a3 * jnp.exp(cum - lw3)
    b_t = b3 * e_inv
    k_t = k3 * e_inv
    r_t = r3 * jnp.exp(cum)
    two = lambda x: jnp.concatenate([x, x], axis=0)
    per_head = lambda x: jnp.concatenate([jnp.where(head0, x, 0.0), jnp.where(head0, 0.0, x)], axis=0)
    lhs = jnp.concatenate([per_head(a_t), per_head(r_t)], axis=1).astype(BF16)
    g_b = jnp.where(g_mask, _bmm_nt(lhs, two(b_t)), 0.0)
    g_k = jnp.where(g_mask, _bmm_nt(lhs, two(k_t)), 0.0)
    a_ab, a_rb = g_b[:, :L], g_b[:, L:]
    a_ak, a_rk = g_k[:, :L], g_k[:, L:]
    t_inv = eye + a_ab
    pw = _bmm(a_ab, a_ab)
    steps = int(math.log2(L)) - 1
    for j in range(steps):
        if j + 1 < steps:
            both = _bmm(jnp.concatenate([t_inv, pw], axis=1), pw)
            t_inv = t_inv + both[:, :L]
            pw = both[:, L:]
        else:
            t_inv = t_inv + _bmm(t_inv, pw)
    v2x = two(v3)
    ta_h = _bmm(t_inv, two(a_t))
    u0_h = _bmm(t_inv, _bmm(a_ak, v2x))
    rp_h = _bmm(a_rb, ta_h)
    y1_h = _bmm(a_rb, u0_h) + _bmm(a_rk, v2x)
    merge = lambda x: jnp.where(head0, x[:C], x[C:])
    flat = lambda x: x.reshape(tb_all, LANES)
    wk_s[0] = flat(merge(ta_h))
    wk_s[1] = flat(merge(u0_h))
    wk_s[2] = flat(r_t + merge(rp_h))
    wk_s[3] = flat(merge(y1_h))
    wk_s[4] = flat(b3 * e_end)
    wk_s[5] = flat(k3 * e_end)
    wk_s[6] = flat(cum)

    diag = lane == sub

    def chunk_maps(c, carry):
        sl = pl.ds(pl.multiple_of(c * L, L), L)
        ta_c, u0_c, bh_c, kh_c = wk_s[0, sl, :], wk_s[1, sl, :], wk_s[4, sl, :], wk_s[5, sl, :]
        tail = wk_s[6, pl.ds(pl.multiple_of(c * L + L - 8, 8), 8), :]
        w_end = jnp.exp(tail[7:8, :])
        m = _sdot_tn(bh_c, ta_c) + jnp.where(diag, w_end, 0.0)
        n = _sdot_tn(bh_c, u0_c) + _sdot_tn(kh_c, st_s[3, sl, :])
        mn_s[c, :, 0:LANES] = jnp.where(same_head, m, 0.0)
        mn_s[c, :, LANES:2 * LANES] = jnp.where(same_head, n, 0.0)
        return carry

    lax.fori_loop(0, C, chunk_maps, 0, unroll=4)

    def chain(c, states):
        new = []
        for u in range(nt):
            cu = c + u * n_chunks
            sl = pl.ds(pl.multiple_of(cu * L, L), L)
            y_s[sl, :] = _dot3(wk_s[2, sl, :], states[u]) + wk_s[3, sl, :]
            new.append(_dot3(mn_s[cu, :, 0:LANES], states[u]) + mn_s[cu, :, LANES:2 * LANES])
        return tuple(new)

    states = lax.fori_loop(0, n_chunks, chain, tuple(p_s[u] for u in range(nt)))
    for u in range(nt):
        p_s[u] = states[u]

    inv_n = 1.0 / RWKV_HEAD
    for u in range(nt):
        cols = slice(u * LANES, (u + 1) * LANES)
        rows = slice(u * tb, (u + 1) * tb)
        y = y_s[rows, :]
        mean = _dot01_right(y, block_ones) * inv_n
        d = y - mean
        var = _dot01_right(d * d, block_ones) * inv_n
        yn = d * lax.rsqrt(var + GN_EPS) * par[9:10, cols] + par[10:11, cols]
        y_out[:, cols] = ((yn + st_s[7, rows, :]) * st_s[6, rows, :]).astype(y_out.dtype)


def rwkv_time_mix(z_r, par, mul, w2p, a2p, g2p, v2p, v_first, *, tb, nt):
    bsz, seq, cols = z_r.shape
    c_rwkv = (cols - LORA_COLS) // 3
    n_groups = c_rwkv // (nt * LANES)
    wide = nt * LANES
    first = v_first is None
    grid = (bsz, n_groups, seq // tb)
    col_spec = lambda off: pl.BlockSpec((None, tb, wide), lambda b, h, t: (b, t, off + h))
    par_rows = par.shape[0]
    in_specs = [
        col_spec(0), col_spec(n_groups), col_spec(2 * n_groups),
        pl.BlockSpec((None, tb, LORA_COLS), lambda b, h, t: (b, t, 3 * c_rwkv // LORA_COLS)),
        pl.BlockSpec((par_rows, wide), lambda b, h, t: (0, h)),
        pl.BlockSpec((1, LORA_COLS), lambda b, h, t: (0, 0)),
        pl.BlockSpec((LANES, wide), lambda b, h, t: (0, h)),
        pl.BlockSpec((LANES, wide), lambda b, h, t: (0, h)),
        pl.BlockSpec((2 * LANES, wide), lambda b, h, t: (0, h)),
        pl.BlockSpec((LANES, wide), lambda b, h, t: (0, h)),
    ]
    args = [z_r, z_r, z_r, z_r, par, mul, w2p, a2p, g2p, v2p]
    y_spec = pl.BlockSpec((None, tb, wide), lambda b, h, t: (b, t, h))
    y_shape = jax.ShapeDtypeStruct((bsz, seq, c_rwkv), BF16)
    if first:
        out_shape = (y_shape, jax.ShapeDtypeStruct((bsz, seq, c_rwkv), F32))
        out_specs = (y_spec, y_spec)
    else:
        in_specs.append(y_spec)
        args.append(v_first)
        out_shape = y_shape
        out_specs = y_spec
    scratch = [
        pltpu.VMEM((nt, LANES, LANES), F32),
        pltpu.VMEM((nt, 8, LANES), F32),
        pltpu.VMEM((1, LORA_COLS), F32),
        pltpu.VMEM((8, nt * tb, LANES), F32),
        pltpu.VMEM((nt * tb, LANES), F32),
        pltpu.VMEM((7, nt * tb, LANES), F32),
        pltpu.VMEM((nt * tb // WKV_CHUNK, LANES, 2 * LANES), F32),
    ]
    return pl.pallas_call(
        functools.partial(_wkv_body, first, tb, nt),
        grid=grid, in_specs=in_specs, out_specs=out_specs, out_shape=out_shape,
        scratch_shapes=scratch, name="rwkv7_time_mix",
        compiler_params=_cparams(("parallel", "parallel", "arbitrary")),
    )(*args)


def _rwkv_params(i, mu_shift, mu_vres, rw_w0, rw_w2, rw_a0, rw_a2, rw_g2, rw_v0, rw_v2,
                 rw_k_k, rw_k_a, rw_r_k, rw_lnx_w, rw_lnx_b):
    c = rw_w0.shape[1]
    mu = mu_shift[i]
    zero_c = jnp.zeros((c,), F32)
    rows = [mu[:c], mu[c:2 * c], mu[2 * c:3 * c], rw_w0[i], rw_a0[i],
            rw_v0[i - 1] if i > 0 else zero_c, rw_k_k[i], rw_k_a[i], rw_r_k[i].reshape(-1),
            rw_lnx_w[i], rw_lnx_b[i]]
    par = jnp.stack(rows + [zero_c] * (16 - len(rows)))
    n_lora = D_DECAY_LORA + D_AAA_LORA + D_GATE_LORA
    mu_v = mu_vres[i - 1] if i > 0 else jnp.zeros((D_MV_LORA,), F32)
    mul = jnp.concatenate([mu[3 * c:], mu_v, jnp.zeros((LORA_COLS - n_lora - D_MV_LORA,), F32)])[None, :]
    w2p = jnp.zeros((LANES, c), F32).at[:D_DECAY_LORA].set(rw_w2[i])
    a2p = jnp.zeros((LANES, c), F32).at[D_DECAY_LORA:D_DECAY_LORA + D_AAA_LORA].set(rw_a2[i])
    g2p = jnp.zeros((2 * LANES, c), F32).at[:D_GATE_LORA].set(rw_g2[i])
    v2p = jnp.zeros((LANES, c), F32)
    if i > 0:
        lv0 = n_lora - 2 * LANES
        v2p = v2p.at[lv0:lv0 + D_MV_LORA].set(rw_v2[i - 1])
    return par, mul, w2p.astype(BF16), a2p.astype(BF16), g2p.astype(BF16), v2p.astype(BF16)


def _rmsnorm_rows(x, g):
    ms = jnp.mean(x * x, axis=-1, keepdims=True)
    return x * lax.rsqrt(ms + RMS_EPS) * g


def _norm_matmul_body(x_ref, g_ref, w_ref, o_ref, u_s):
    @pl.when(pl.program_id(1) == 0)
    def _():
        u_s[...] = _rmsnorm_rows(x_ref[...], g_ref[...]).astype(BF16)

    o_ref[...] = jnp.dot(u_s[...], w_ref[...], preferred_element_type=F32).astype(o_ref.dtype)


def norm_matmul(x, g, w, out_dtype, *, tm, tn):
    n, d = x.shape
    c = w.shape[1]
    return pl.pallas_call(
        _norm_matmul_body,
        grid=(n // tm, c // tn),
        in_specs=[pl.BlockSpec((tm, d), lambda i, j: (i, 0)),
                  pl.BlockSpec((1, d), lambda i, j: (0, 0)),
                  pl.BlockSpec((d, tn), lambda i, j: (0, j))],
        out_specs=pl.BlockSpec((tm, tn), lambda i, j: (i, j)),
        out_shape=jax.ShapeDtypeStruct((n, c), out_dtype),
        scratch_shapes=[pltpu.VMEM((tm, d), BF16)],
        name="norm_matmul",
        compiler_params=_cparams(("parallel", "arbitrary")),
    )(x, g, w)


NEG_BIG = -1e30


def _diff_attn_body(tq, lam_init, lp_ref, g_ref, q_ref, kt_ref, v_ref, o_ref):
    qi = pl.program_id(2)
    d = DIFF_HEAD_DIM
    lp = lp_ref[...]
    lam = (jnp.exp(jnp.sum(lp[0:1] * lp[1:2], axis=-1, keepdims=True))
           - jnp.exp(jnp.sum(lp[2:3] * lp[3:4], axis=-1, keepdims=True)) + lam_init)
    q = q_ref[...] * jnp.asarray(d ** -0.5, BF16)
    lane = lax.broadcasted_iota(jnp.int32, q.shape, 1)
    q_maps = (jnp.where(lane < d, q, jnp.zeros_like(q)), jnp.where(lane >= d, q, jnp.zeros_like(q)))

    def step(j, carry, masked):
        start = pl.multiple_of(j * tq, tq)
        kt = kt_ref[:, pl.ds(start, tq)]
        vb = v_ref[pl.ds(start, tq), :]
        new = []
        for c in range(2):
            m, l, acc = carry[c]
            s = jnp.dot(q_maps[c], kt, preferred_element_type=F32)
            if masked:
                row = lax.broadcasted_iota(jnp.int32, s.shape, 0)
                col = lax.broadcasted_iota(jnp.int32, s.shape, 1)
                s = jnp.where(col <= row, s, NEG_BIG)
            m_new = jnp.maximum(m, jnp.max(s, axis=-1, keepdims=True))
            alpha = jnp.exp(m - m_new)
            pr = jnp.exp(s - m_new)
            l_new = alpha * l + jnp.sum(pr, axis=-1, keepdims=True)
            acc_new = alpha * acc + jnp.dot(pr.astype(BF16), vb, preferred_element_type=F32)
            new.append((m_new, l_new, acc_new))
        return tuple(new)

    init_one = (jnp.full((tq, 1), NEG_BIG, F32), jnp.zeros((tq, 1), F32), jnp.zeros((tq, LANES), F32))
    carry = lax.fori_loop(0, qi, lambda j, c: step(j, c, False), (init_one, init_one))
    (_, l1, acc1), (_, l2, acc2) = step(qi, carry, True)
    o = acc1 / l1 - lam * (acc2 / l2)
    o = o * lax.rsqrt(jnp.mean(o * o, axis=-1, keepdims=True) + SUBLN_EPS)
    o_ref[...] = (o * g_ref[...] * (1.0 - lam_init)).astype(o_ref.dtype)


def diff_attention(qkv, kt, lam_params, subln_g, lam_init, *, tq):
    bsz, seq, c3 = qkv.shape
    n_heads = c3 // (3 * LANES)
    return pl.pallas_call(
        functools.partial(_diff_attn_body, tq, lam_init),
        grid=(bsz, n_heads, seq // tq),
        in_specs=[pl.BlockSpec(lam_params.shape, lambda b, h, i: (0, 0)),
                  pl.BlockSpec((1, LANES), lambda b, h, i: (0, 0)),
                  pl.BlockSpec((None, tq, LANES), lambda b, h, i: (b, i, h)),
                  pl.BlockSpec((None, None, LANES, seq), lambda b, h, i: (b, h, 0, 0)),
                  pl.BlockSpec((None, seq, LANES), lambda b, h, i: (b, 0, 2 * n_heads + h))],
        out_specs=pl.BlockSpec((None, tq, LANES), lambda b, h, i: (b, i, h)),
        out_shape=jax.ShapeDtypeStruct((bsz, seq, n_heads * LANES), BF16),
        name="diff_attention",
        compiler_params=_cparams(("parallel", "parallel", "arbitrary")),
    )(lam_params, subln_g, qkv, kt, qkv)


def _out_proj_body(ya_ref, yb_ref, wa_ref, wb_ref, h_ref, o_ref):
    acc = jnp.dot(ya_ref[...], wa_ref[...], preferred_element_type=F32)
    acc = acc + jnp.dot(yb_ref[...], wb_ref[...], preferred_element_type=F32)
    o_ref[...] = h_ref[...] + acc


def out_proj(ya, yb, wa, wb, h, *, tm, tn):
    n, d = h.shape
    ca, cb = ya.shape[1], yb.shape[1]
    return pl.pallas_call(
        _out_proj_body,
        grid=(n // tm, d // tn),
        in_specs=[pl.BlockSpec((tm, ca), lambda i, j: (i, 0)),
                  pl.BlockSpec((tm, cb), lambda i, j: (i, 0)),
                  pl.BlockSpec((ca, tn), lambda i, j: (0, j)),
                  pl.BlockSpec((cb, tn), lambda i, j: (0, j)),
                  pl.BlockSpec((tm, tn), lambda i, j: (i, j))],
        out_specs=pl.BlockSpec((tm, tn), lambda i, j: (i, j)),
        out_shape=jax.ShapeDtypeStruct((n, d), F32),
        name="out_proj",
        compiler_params=_cparams(("parallel", "parallel")),
    )(ya, yb, wa, wb, h)


def _swiglu_tile(u, w1_ref, w3_ref, w2_ref):
    a = jnp.dot(u, w1_ref[...], preferred_element_type=F32)
    b = jnp.dot(u, w3_ref[...], preferred_element_type=F32)
    mid = (a * _sigmoid(a) * b).astype(BF16)
    return jnp.dot(mid, w2_ref[...], preferred_element_type=F32)


def _dense_ffn_body(x_ref, g_ref, w1_ref, w3_ref, w2_ref, o_ref, u_s, acc_s):
    f = pl.program_id(1)

    @pl.when(f == 0)
    def _():
        u_s[...] = _rmsnorm_rows(x_ref[...], g_ref[...]).astype(BF16)
        acc_s[...] = jnp.zeros_like(acc_s)

    acc_s[...] += _swiglu_tile(u_s[...], w1_ref, w3_ref, w2_ref)

    @pl.when(f == pl.num_programs(1) - 1)
    def _():
        o_ref[...] = x_ref[...] + acc_s[...]


def dense_ffn(x, g, w1, w3, w2, *, tm, tf):
    n, d = x.shape
    ff = w1.shape[1]
    return pl.pallas_call(
        _dense_ffn_body,
        grid=(n // tm, ff // tf),
        in_specs=[pl.BlockSpec((tm, d), lambda i, f: (i, 0)),
                  pl.BlockSpec((1, d), lambda i, f: (0, 0)),
                  pl.BlockSpec((d, tf), lambda i, f: (0, f)),
                  pl.BlockSpec((d, tf), lambda i, f: (0, f)),
                  pl.BlockSpec((tf, d), lambda i, f: (f, 0))],
        out_specs=pl.BlockSpec((tm, d), lambda i, f: (i, 0)),
        out_shape=jax.ShapeDtypeStruct((n, d), F32),
        scratch_shapes=[pltpu.VMEM((tm, d), BF16), pltpu.VMEM((tm, d), F32)],
        name="dense_ffn",
        compiler_params=_cparams(("parallel", "arbitrary")),
    )(x, g, w1, w3, w2)


ROUTE_ROWS = 8


def _router_body(n_experts, x_ref, g_ref, wr_ref, u_ref, route_ref):
    u = _rmsnorm_rows(x_ref[...], g_ref[...])
    u_ref[...] = u.astype(BF16)
    u_hi, u_lo = _split2(u)
    wr = wr_ref[...]
    w_hi, w_lo = _split2(wr)
    logits = (jnp.dot(u_hi, w_hi, preferred_element_type=F32) + jnp.dot(u_hi, w_lo, preferred_element_type=F32)
              + jnp.dot(u_lo, w_hi, preferred_element_type=F32))
    lane = lax.broadcasted_iota(jnp.int32, logits.shape, 1)
    logits = jnp.where(lane < n_experts, logits, NEG_BIG)
    top1 = jnp.max(logits, axis=-1, keepdims=True)
    idx1 = jnp.min(jnp.where(logits == top1, lane, LANES), axis=-1, keepdims=True)
    rest = jnp.where(lane == idx1, NEG_BIG, logits)
    top2 = jnp.max(rest, axis=-1, keepdims=True)
    idx2 = jnp.min(jnp.where(rest == top2, lane, LANES), axis=-1, keepdims=True)
    e2 = jnp.exp(top2 - top1)
    gate1 = 1.0 / (1.0 + e2)
    slab = jnp.where(lane == 0, idx1.astype(F32),
                     jnp.where(lane == 1, idx2.astype(F32),
                               jnp.where(lane == 2, gate1, jnp.where(lane == 3, e2 * gate1, 0.0))))
    route_ref[...] = slab.T[:ROUTE_ROWS, :]


def moe_router(x, g, router_w, *, tm):
    n, d = x.shape
    n_experts = router_w.shape[1]
    wr = jnp.zeros((d, LANES), F32).at[:, :n_experts].set(router_w)
    return pl.pallas_call(
        functools.partial(_router_body, n_experts),
        grid=(n // tm,),
        in_specs=[pl.BlockSpec((tm, d), lambda i: (i, 0)),
                  pl.BlockSpec((1, d), lambda i: (0, 0)),
                  pl.BlockSpec((d, LANES), lambda i: (0, 0))],
        out_specs=(pl.BlockSpec((tm, d), lambda i: (i, 0)),
                   pl.BlockSpec((ROUTE_ROWS, tm), lambda i: (0, i))),
        out_shape=(jax.ShapeDtypeStruct((n, d), BF16),
                   jax.ShapeDtypeStruct((ROUTE_ROWS, n), F32)),
        name="moe_router",
        compiler_params=_cparams(("parallel",)),
    )(x, g, wr)


def _moe_ffn_body(be_ref, nu_ref, x_ref, gate_ref, w1_ref, w3_ref, w2_ref, o_ref, acc_s):
    i, f = pl.program_id(0), pl.program_id(1)
    used = i < nu_ref[0]

    @pl.when(jnp.logical_and(used, f == 0))
    def _():
        acc_s[...] = jnp.zeros_like(acc_s)

    @pl.when(used)
    def _():
        acc_s[...] += _swiglu_tile(x_ref[...], w1_ref, w3_ref, w2_ref)

    @pl.when(f == pl.num_programs(1) - 1)
    def _():
        o_ref[...] = jnp.where(used, acc_s[...] * gate_ref[...], 0.0)


def moe_ffn(x_rows, row_gate, blk_expert, n_used, w1, w3, w2, *, tm, tf):
    rows, d = x_rows.shape
    ff = w1.shape[2]
    nf = ff // tf

    def w_idx(i, f, be, nu):
        return be[i], jnp.where(i < nu[0], f, nf - 1)

    grid_spec = pltpu.PrefetchScalarGridSpec(
        num_scalar_prefetch=2,
        grid=(rows // tm, nf),
        in_specs=[pl.BlockSpec((tm, d), lambda i, f, be, nu: (i, 0)),
                  pl.BlockSpec((tm, 1), lambda i, f, be, nu: (i, 0)),
                  pl.BlockSpec((None, d, tf), lambda i, f, be, nu: (w_idx(i, f, be, nu)[0], 0, w_idx(i, f, be, nu)[1])),
                  pl.BlockSpec((None, d, tf), lambda i, f, be, nu: (w_idx(i, f, be, nu)[0], 0, w_idx(i, f, be, nu)[1])),
                  pl.BlockSpec((None, tf, d), lambda i, f, be, nu: (w_idx(i, f, be, nu)[0], w_idx(i, f, be, nu)[1], 0))],
        out_specs=pl.BlockSpec((tm, d), lambda i, f, be, nu: (i, 0)),
        scratch_shapes=[pltpu.VMEM((tm, d), F32)],
    )
    return pl.pallas_call(
        _moe_ffn_body, grid_spec=grid_spec,
        out_shape=jax.ShapeDtypeStruct((rows, d), F32),
        name="moe_ffn",
        compiler_params=_cparams(("arbitrary", "arbitrary")),
    )(blk_expert, n_used, x_rows, row_gate, w1, w3, w2)


def moe_layer(h, g, router_w, w1, w3, w2, *, tm_router, tm, tf):
    n, d = h.shape
    n_experts = router_w.shape[1]
    u, route = moe_router(h, g, router_w, tm=tm_router)
    e_flat = route[:TOP_K].T.reshape(-1).astype(jnp.int32)
    g_flat = route[TOP_K:2 * TOP_K].T.reshape(-1)
    m = n * TOP_K
    onehot = (e_flat[:, None] == jnp.arange(n_experts, dtype=jnp.int32)[None, :]).astype(jnp.int32)
    counts = jnp.sum(onehot, axis=0)
    rank = jnp.sum((jnp.cumsum(onehot, axis=0) - onehot) * onehot, axis=1)
    padded = (counts + tm - 1) // tm * tm
    pad_ends = jnp.cumsum(padded)
    dest = (pad_ends - padded)[e_flat] + rank
    n_blocks = m // tm + n_experts
    rows = n_blocks * tm
    row_tok = jnp.zeros((rows,), jnp.int32).at[dest].set(jnp.arange(m, dtype=jnp.int32) // TOP_K)
    row_gate = jnp.zeros((rows,), F32).at[dest].set(g_flat)
    blk_start = jnp.arange(n_blocks, dtype=jnp.int32) * tm
    blk_expert = jnp.minimum(jnp.sum(blk_start[:, None] >= pad_ends[None, :], axis=1), n_experts - 1).astype(jnp.int32)
    n_used = (pad_ends[-1:] // tm).astype(jnp.int32)
    y_rows = moe_ffn(u[row_tok], row_gate[:, None], blk_expert, n_used, w1, w3, w2, tm=tm, tf=tf)
    return y_rows[dest].reshape(n, TOP_K * d)


def _ple_body(final_norm, n_add, x_ref, *refs):
    add_ref = refs[0] if n_add else None
    g_ref, wg_ref, p_ref, wp_ref, gf_ref, o_ref = refs[1 if n_add else 0:]
    x = x_ref[...]
    d = x.shape[1]
    for k in range(n_add):
        x = x + add_ref[:, k * d:(k + 1) * d]
    u = _rmsnorm_rows(x, g_ref[...]).astype(BF16)
    gate = _sigmoid(jnp.dot(u, wg_ref[...], preferred_element_type=F32))
    proj = jnp.dot(p_ref[...].astype(BF16), wp_ref[...], preferred_element_type=F32)
    out = x + gate * proj
    if final_norm:
        out = _rmsnorm_rows(out, gf_ref[...])
    o_ref[...] = out


def ple_layer(x, addends, g, w_gate, p, w_proj, g_final, final_norm, *, tm):
    n, d = x.shape
    dp = p.shape[1]
    n_add = 0 if addends is None else addends.shape[1] // d
    add_specs = [pl.BlockSpec((tm, n_add * d), lambda i: (i, 0))] if n_add else []
    add_args = [addends] if n_add else []
    return pl.pallas_call(
        functools.partial(_ple_body, final_norm, n_add),
        grid=(n // tm,),
        in_specs=[pl.BlockSpec((tm, d), lambda i: (i, 0))] + add_specs + [
                  pl.BlockSpec((1, d), lambda i: (0, 0)),
                  pl.BlockSpec((d, d), lambda i: (0, 0)),
                  pl.BlockSpec((tm, dp), lambda i: (i, 0)),
                  pl.BlockSpec((dp, d), lambda i: (0, 0)),
                  pl.BlockSpec((1, d), lambda i: (0, 0))],
        out_specs=pl.BlockSpec((tm, d), lambda i: (i, 0)),
        out_shape=jax.ShapeDtypeStruct((n, d), F32),
        name="ple_layer",
        compiler_params=_cparams(("parallel",)),
    )(x, *add_args, g, w_gate, p, w_proj, g_final)


def _lambda_init(layer_idx):
    return 0.8 - 0.6 * math.exp(-0.3 * layer_idx)


def _tile(n, pref):
    t = min(n, pref)
    while n % t:
        t //= 2
    return t


def kernel(x, p, g_mix, w_in, w_vres_down, mu_shift, mu_vres, rw_w0, rw_w2, rw_a0, rw_a2, rw_g2, rw_v0, rw_v2, rw_k_k, rw_k_a, rw_r_k, rw_lnx_w, rw_lnx_b, diff_lambda, diff_subln, w_out, g_ffn, dense_w1, dense_w3, dense_w2, router_w, moe_w1, moe_w3, moe_w2, g_ple, w_ple_gate, w_ple_proj, g_final):
    bsz, seq, d = x.shape
    depth = w_in.shape[0]
    n = bsz * seq
    c_rwkv = rw_w0.shape[1]
    shift_cols = mu_shift.shape[1]
    zr_cols = 3 * c_rwkv + LORA_COLS
    row = lambda v: v.reshape(1, -1)
    tm = _tile(n, 1024)
    h = x.reshape(n, d)
    v_first = None
    for i in range(depth):
        w_vres = w_vres_down[i - 1] if i > 0 else jnp.zeros((d, D_MV_LORA), F32)
        w_r = jnp.concatenate([w_in[i][:, :shift_cols], w_vres,
                               jnp.zeros((d, zr_cols - shift_cols - D_MV_LORA), F32)], axis=1).astype(BF16)
        w_d = w_in[i][:, shift_cols:].astype(BF16)
        z_r = norm_matmul(h, row(g_mix[i]), w_r, F32, tm=tm, tn=zr_cols // 3)
        qkv = norm_matmul(h, row(g_mix[i]), w_d, BF16, tm=tm, tn=w_d.shape[1] // 3)
        packed = _rwkv_params(i, mu_shift, mu_vres, rw_w0, rw_w2, rw_a0, rw_a2, rw_g2, rw_v0, rw_v2,
                              rw_k_k, rw_k_a, rw_r_k, rw_lnx_w, rw_lnx_b)
        mixed = rwkv_time_mix(z_r.reshape(bsz, seq, zr_cols), *packed, v_first, tb=_tile(seq, 512), nt=4)
        if i == 0:
            y_rwkv, v_first = mixed
        else:
            y_rwkv = mixed
        qkv = qkv.reshape(bsz, seq, -1)
        c_diff = qkv.shape[-1] // 3
        kt = qkv[:, :, c_diff:2 * c_diff].reshape(bsz, seq, c_diff // LANES, LANES).transpose(0, 2, 3, 1)
        y_diff = diff_attention(qkv, kt, diff_lambda[i], row(diff_subln[i]), _lambda_init(i), tq=_tile(seq, 1024))
        wo = w_out[i].astype(BF16)
        h = out_proj(y_rwkv.reshape(n, c_rwkv), y_diff.reshape(n, -1), wo[:c_rwkv], wo[c_rwkv:], h,
                     tm=tm, tn=_tile(d, 1024))
        expert_out = None
        if i % 2 == 0:
            j = i // 2
            h = dense_ffn(h, row(g_ffn[i]), dense_w1[j].astype(BF16), dense_w3[j].astype(BF16),
                          dense_w2[j].astype(BF16), tm=_tile(n, 512), tf=512)
        else:
            j = i // 2
            expert_out = moe_layer(h, row(g_ffn[i]), router_w[j], moe_w1[j].astype(BF16), moe_w3[j].astype(BF16),
                                   moe_w2[j].astype(BF16), tm_router=_tile(n, 512), tm=_tile(n, 1024), tf=512)
        h = ple_layer(h, expert_out, row(g_ple[i]), w_ple_gate[i].astype(BF16), p[i].reshape(n, -1),
                      w_ple_proj[i].astype(BF16), row(g_final), i == depth - 1,
                      tm=_tile(n, 512 if expert_out is None else 256))
    return h.reshape(bsz, seq, d)
```

```python
import functools
import math

import jax
import jax.numpy as jnp
from jax import lax
from jax.experimental import pallas as pl
from jax.experimental.pallas import tpu as pltpu

F32 = jnp.float32
BF16 = jnp.bfloat16

LANES = 128
VMEM_LIMIT = 56 * 1024 * 1024

RWKV_HEAD = 64
HEADS_PER_TILE = LANES // RWKV_HEAD
DIFF_HEAD_DIM = 64
D_DECAY_LORA, D_AAA_LORA, D_MV_LORA, D_GATE_LORA = 64, 64, 32, 160
LORA_COLS = 3 * LANES
TOP_K = 2
RMS_EPS = 1e-6
GN_EPS = 64e-5
SUBLN_EPS = 1e-5

WKV_CHUNK = 64


def _cparams(sem):
    return pltpu.CompilerParams(dimension_semantics=sem, vmem_limit_bytes=VMEM_LIMIT)


def _dot01_right(x, m01, nsplit=3):
    acc, rem = None, x
    for _ in range(nsplit):
        piece = rem.astype(BF16)
        t = jnp.dot(piece, m01, preferred_element_type=F32)
        acc = t if acc is None else acc + t
        rem = rem - piece.astype(F32)
    return acc


def _split2(x):
    hi = x.astype(BF16)
    return hi, (x - hi.astype(F32)).astype(BF16)


def _sdot_tn(a, b):
    return lax.dot_general(a.astype(BF16), b.astype(BF16), (((0,), (0,)), ((), ())), preferred_element_type=F32)


def _bmm(a, b):
    return lax.dot_general(a.astype(BF16), b.astype(BF16), (((2,), (1,)), ((0,), (0,))), preferred_element_type=F32)


def _bmm_nt(a, b):
    return lax.dot_general(a.astype(BF16), b.astype(BF16), (((2,), (2,)), ((0,), (0,))), preferred_element_type=F32)


def _bdot01(m01, x, nsplit):
    acc, rem = None, x
    for _ in range(nsplit):
        piece = rem.astype(BF16)
        t = lax.dot_general(m01, piece, (((2,), (1,)), ((0,), (0,))), preferred_element_type=F32)
        acc = t if acc is None else acc + t
        rem = rem - piece.astype(F32)
    return acc


def _dot3(a, b):
    dn = (((1,), (0,)), ((), ()))
    a_hi, a_lo = _split2(a)
    b_hi, b_lo = _split2(b)
    out = lax.dot_general(a_hi, b_hi, dn, preferred_element_type=F32)
    out = out + lax.dot_general(a_hi, b_lo, dn, preferred_element_type=F32)
    return out + lax.dot_general(a_lo, b_hi, dn, preferred_element_type=F32)


def _token_shift(x, prev_row, mu):
    rolled = pltpu.roll(x, 1, 0)
    row = lax.broadcasted_iota(jnp.int32, x.shape, 0)
    prev = jnp.where(row == 0, prev_row, rolled)
    return x + mu * (prev - x)


def _sigmoid(x):
    return 1.0 / (1.0 + jnp.exp(-x))


def _wkv_body(first_layer, tb, nt, *refs):
    if first_layer:
        (zr, zk, zv, zl, par, mul, w2, a2, g2, v2,
         y_out, vf_out, p_s, prev_s, prevl_s, st_s, y_s, wk_s, mn_s) = refs
        vf_in = None
    else:
        (zr, zk, zv, zl, par, mul, w2, a2, g2, v2, vf_in,
         y_out, p_s, prev_s, prevl_s, st_s, y_s, wk_s, mn_s) = refs
    L = WKV_CHUNK
    n_chunks = tb // L

    @pl.when(pl.program_id(2) == 0)
    def _():
        p_s[...] = jnp.zeros_like(p_s)
        prev_s[...] = jnp.zeros_like(prev_s)
        prevl_s[...] = jnp.zeros_like(prevl_s)

    lane = lax.broadcasted_iota(jnp.int32, (LANES, LANES), 1)
    sub = lax.broadcasted_iota(jnp.int32, (LANES, LANES), 0)
    same_head = (lane // RWKV_HEAD) == (sub // RWKV_HEAD)
    block_ones = jnp.where(same_head, 1.0, 0.0).astype(BF16)

    zl_v = zl[...]
    ls = _token_shift(zl_v, prevl_s[...], mul[...])
    prevl_s[...] = zl_v[tb - 1:tb, :]
    l0 = ls[:, 0:LANES]
    l0_tanh = jnp.tanh(l0).astype(BF16)
    l0 = l0.astype(BF16)
    l1_sig = _sigmoid(ls[:, LANES:3 * LANES]).astype(BF16)
    l2 = ls[:, 2 * LANES:3 * LANES].astype(BF16)
    for u in range(nt):
        cols = slice(u * LANES, (u + 1) * LANES)
        rows = slice(u * tb, (u + 1) * tb)
        zr_v, zk_v, zv_v = zr[:, cols], zk[:, cols], zv[:, cols]
        r = _token_shift(zr_v, prev_s[u, 0:1, :], par[0:1, cols])
        k = _token_shift(zk_v, prev_s[u, 1:2, :], par[1:2, cols])
        v = _token_shift(zv_v, prev_s[u, 2:3, :], par[2:3, cols])
        prev_s[u, 0:1, :] = zr_v[tb - 1:tb, :]
        prev_s[u, 1:2, :] = zk_v[tb - 1:tb, :]
        prev_s[u, 2:3, :] = zv_v[tb - 1:tb, :]
        wl = par[3:4, cols] + jnp.dot(l0_tanh, w2[:, cols], preferred_element_type=F32)
        al = par[4:5, cols] + jnp.dot(l0, a2[:, cols], preferred_element_type=F32)
        gate = jnp.dot(l1_sig, g2[:, cols], preferred_element_type=F32)
        if first_layer:
            vf_out[:, cols] = v
        else:
            vg = _sigmoid(par[5:6, cols] + jnp.dot(l2, v2[:, cols], preferred_element_type=F32))
            v = v + (vf_in[:, cols] - v) * vg
        neg_wl = -wl
        softplus = jnp.maximum(neg_wl, 0.0) + jnp.log(1.0 + jnp.exp(-jnp.abs(neg_wl)))
        logw = -jnp.exp(-softplus - 0.5)
        asig = _sigmoid(al)
        kk = k * par[6:7, cols]
        kk_norm = jnp.sqrt(_dot01_right(kk * kk, block_ones))
        kk = kk / jnp.maximum(kk_norm, 1e-12)
        k2 = k * (1.0 + (asig - 1.0) * par[7:8, cols])
        bonus = _dot01_right(r * k2 * par[8:9, cols], block_ones) * v
        st_s[0, rows, :] = r
        st_s[1, rows, :] = logw
        st_s[2, rows, :] = k2
        st_s[3, rows, :] = v
        st_s[4, rows, :] = -kk
        st_s[5, rows, :] = kk * asig
        st_s[6, rows, :] = gate
        st_s[7, rows, :] = bonus

    C = nt * n_chunks
    tb_all = nt * tb
    ti = lax.broadcasted_iota(jnp.int32, (C, L, L), 1)
    si = lax.broadcasted_iota(jnp.int32, (C, L, L), 2)
    tri01 = jnp.where(si <= ti, 1.0, 0.0).astype(BF16)
    ti2 = lax.broadcasted_iota(jnp.int32, (2 * C, 2 * L, L), 1)
    si2 = lax.broadcasted_iota(jnp.int32, (2 * C, 2 * L, L), 2)
    g_mask = si2 <= jnp.where(ti2 < L, ti2 - 1, ti2 - L)
    eye = jnp.where(lax.broadcasted_iota(jnp.int32, (2 * C, L, L), 1)
                    == lax.broadcasted_iota(jnp.int32, (2 * C, L, L), 2), 1.0, 0.0).astype(F32)
    head0 = lax.broadcasted_iota(jnp.int32, (C, L, LANES), 2) < RWKV_HEAD

    chunks = lambda x: x.reshape(C, L, LANES)
    r3, lw3, k3, v3, a3, b3 = (chunks(st_s[i]) for i in range(6))
    cum = _bdot01(tri01, lw3, nsplit=2)
    c_last = cum[:, L - 1:L, :]
    e_inv = jnp.exp(-cum)
    e_end = jnp.exp(c_last - cum)
    a_t = a3 * jnp.exp(cum - lw3)
    b_t = b3 * e_inv
    k_t = k3 * e_inv
    r_t = r3 * jnp.exp(cum)
    two = lambda x: jnp.concatenate([x, x], axis=0)
    per_head = lambda x: jnp.concatenate([jnp.where(head0, x, 0.0), jnp.where(head0, 0.0, x)], axis=0)
    lhs = jnp.concatenate([per_head(a_t), per_head(r_t)], axis=1).astype(BF16)
    g_b = jnp.where(g_mask, _bmm_nt(lhs, two(b_t)), 0.0)
    g_k = jnp.where(g_mask, _bmm_nt(lhs, two(k_t)), 0.0)
    a_ab, a_rb = g_b[:, :L], g_b[:, L:]
    a_ak, a_rk = g_k[:, :L], g_k[:, L:]
    t_inv = eye + a_ab
    pw = _bmm(a_ab, a_ab)
    steps = int(math.log2(L)) - 1
    for j in range(steps):
        if j + 1 < steps:
            both = _bmm(jnp.concatenate([t_inv, pw], axis=1), pw)
            t_inv = t_inv + both[:, :L]
            pw = both[:, L:]
        else:
            t_inv = t_inv + _bmm(t_inv, pw)
    v2x = two(v3)
    ta_h = _bmm(t_inv, two(a_t))
    u0_h = _bmm(t_inv, _bmm(a_ak, v2x))
    rp_h = _bmm(a_rb, ta_h)
    y1_h = _bmm(a_rb, u0_h) + _bmm(a_rk, v2x)
    merge = lambda x: jnp.where(head0, x[:C], x[C:])
    flat = lambda x: x.reshape(tb_all, LANES)
    wk_s[0] = flat(merge(ta_h))
    wk_s[1] = flat(merge(u0_h))
    wk_s[2] = flat(r_t + merge(rp_h))
    wk_s[3] = flat(merge(y1_h))
    wk_s[4] = flat(b3 * e_end)
    wk_s[5] = flat(k3 * e_end)
    wk_s[6] = flat(cum)

    diag = lane == sub

    def chunk_maps(c, carry):
        sl = pl.ds(pl.multiple_of(c * L, L), L)
        ta_c, u0_c, bh_c, kh_c = wk_s[0, sl, :], wk_s[1, sl, :], wk_s[4, sl, :], wk_s[5, sl, :]
        tail = wk_s[6, pl.ds(pl.multiple_of(c * L + L - 8, 8), 8), :]
        w_end = jnp.exp(tail[7:8, :])
        m = _sdot_tn(bh_c, ta_c) + jnp.where(diag, w_end, 0.0)
        n = _sdot_tn(bh_c, u0_c) + _sdot_tn(kh_c, st_s[3, sl, :])
        mn_s[c, :, 0:LANES] = jnp.where(same_head, m, 0.0)
        mn_s[c, :, LANES:2 * LANES] = jnp.where(same_head, n, 0.0)
        return carry

    lax.fori_loop(0, C, chunk_maps, 0, unroll=4)

    def chain(c, states):
        new = []
        for u in range(nt):
            cu = c + u * n_chunks
            sl = pl.ds(pl.multiple_of(cu * L, L), L)
            y_s[sl, :] = _dot3(wk_s[2, sl, :], states[u]) + wk_s[3, sl, :]
            new.append(_dot3(mn_s[cu, :, 0:LANES], states[u]) + mn_s[cu, :, LANES:2 * LANES])
        return tuple(new)

    states = lax.fori_loop(0, n_chunks, chain, tuple(p_s[u] for u in range(nt)))
    for u in range(nt):
        p_s[u] = states[u]

    inv_n = 1.0 / RWKV_HEAD
    for u in range(nt):
        cols = slice(u * LANES, (u + 1) * LANES)
        rows = slice(u * tb, (u + 1) * tb)
        y = y_s[rows, :]
        mean = _dot01_right(y, block_ones) * inv_n
        d = y - mean
        var = _dot01_right(d * d, block_ones) * inv_n
        yn = d * lax.rsqrt(var + GN_EPS) * par[9:10, cols] + par[10:11, cols]
        y_out[:, cols] = ((yn + st_s[7, rows, :]) * st_s[6, rows, :]).astype(y_out.dtype)


def rwkv_time_mix(z_r, par, mul, w2p, a2p, g2p, v2p, v_first, *, tb, nt):
    bsz, seq, cols = z_r.shape
    c_rwkv = (cols - LORA_COLS) // 3
    n_groups = c_rwkv // (nt * LANES)
    wide = nt * LANES
    first = v_first is None
    grid = (bsz, n_groups, seq // tb)
    col_spec = lambda off: pl.BlockSpec((None, tb, wide), lambda b, h, t: (b, t, off + h))
    par_rows = par.shape[0]
    in_specs = [
        col_spec(0), col_spec(n_groups), col_spec(2 * n_groups),
        pl.BlockSpec((None, tb, LORA_COLS), lambda b, h, t: (b, t, 3 * c_rwkv // LORA_COLS)),
        pl.BlockSpec((par_rows, wide), lambda b, h, t: (0, h)),
        pl.BlockSpec((1, LORA_COLS), lambda b, h, t: (0, 0)),
        pl.BlockSpec((LANES, wide), lambda b, h, t: (0, h)),
        pl.BlockSpec((LANES, wide), lambda b, h, t: (0, h)),
        pl.BlockSpec((2 * LANES, wide), lambda b, h, t: (0, h)),
        pl.BlockSpec((LANES, wide), lambda b, h, t: (0, h)),
    ]
    args = [z_r, z_r, z_r, z_r, par, mul, w2p, a2p, g2p, v2p]
    y_spec = pl.BlockSpec((None, tb, wide), lambda b, h, t: (b, t, h))
    y_shape = jax.ShapeDtypeStruct((bsz, seq, c_rwkv), BF16)
    if first:
        out_shape = (y_shape, jax.ShapeDtypeStruct((bsz, seq, c_rwkv), F32))
        out_specs = (y_spec, y_spec)
    else:
        in_specs.append(y_spec)
        args.append(v_first)
        out_shape = y_shape
        out_specs = y_spec
    scratch = [
        pltpu.VMEM((nt, LANES, LANES), F32),
        pltpu.VMEM((nt, 8, LANES), F32),
        pltpu.VMEM((1, LORA_COLS), F32),
        pltpu.VMEM((8, nt * tb, LANES), F32),
        pltpu.VMEM((nt * tb, LANES), F32),
        pltpu.VMEM((7, nt * tb, LANES), F32),
        pltpu.VMEM((nt * tb // WKV_CHUNK, LANES, 2 * LANES), F32),
    ]
    return pl.pallas_call(
        functools.partial(_wkv_body, first, tb, nt),
        grid=grid, in_specs=in_specs, out_specs=out_specs, out_shape=out_shape,
        scratch_shapes=scratch, name="rwkv7_time_mix",
        compiler_params=_cparams(("parallel", "parallel", "arbitrary")),
    )(*args)


def _rwkv_params(i, mu_shift, mu_vres, rw_w0, rw_w2, rw_a0, rw_a2, rw_g2, rw_v0, rw_v2,
                 rw_k_k, rw_k_a, rw_r_k, rw_lnx_w, rw_lnx_b):
    c = rw_w0.shape[1]
    mu = mu_shift[i]
    zero_c = jnp.zeros((c,), F32)
    rows = [mu[:c], mu[c:2 * c], mu[2 * c:3 * c], rw_w0[i], rw_a0[i],
            rw_v0[i - 1] if i > 0 else zero_c, rw_k_k[i], rw_k_a[i], rw_r_k[i].reshape(-1),
            rw_lnx_w[i], rw_lnx_b[i]]
    par = jnp.stack(rows + [zero_c] * (16 - len(rows)))
    n_lora = D_DECAY_LORA + D_AAA_LORA + D_GATE_LORA
    mu_v = mu_vres[i - 1] if i > 0 else jnp.zeros((D_MV_LORA,), F32)
    mul = jnp.concatenate([mu[3 * c:], mu_v, jnp.zeros((LORA_COLS - n_lora - D_MV_LORA,), F32)])[None, :]
    w2p = jnp.zeros((LANES, c), F32).at[:D_DECAY_LORA].set(rw_w2[i])
    a2p = jnp.zeros((LANES, c), F32).at[D_DECAY_LORA:D_DECAY_LORA + D_AAA_LORA].set(rw_a2[i])
    g2p = jnp.zeros((2 * LANES, c), F32).at[:D_GATE_LORA].set(rw_g2[i])
    v2p = jnp.zeros((LANES, c), F32)
    if i > 0:
        lv0 = n_lora - 2 * LANES
        v2p = v2p.at[lv0:lv0 + D_MV_LORA].set(rw_v2[i - 1])
    return par, mul, w2p.astype(BF16), a2p.astype(BF16), g2p.astype(BF16), v2p.astype(BF16)


def _rmsnorm_rows(x, g):
    ms = jnp.mean(x * x, axis=-1, keepdims=True)
    return x * lax.rsqrt(ms + RMS_EPS) * g


def _norm_matmul_body(x_ref, g_ref, w_ref, o_ref, u_s):
    @pl.when(pl.program_id(1) == 0)
    def _():
        u_s[...] = _rmsnorm_rows(x_ref[...], g_ref[...]).astype(BF16)

    o_ref[...] = jnp.dot(u_s[...], w_ref[...], preferred_element_type=F32).astype(o_ref.dtype)


def norm_matmul(x, g, w, out_dtype, *, tm, tn):
    n, d = x.shape
    c = w.shape[1]
    return pl.pallas_call(
        _norm_matmul_body,
        grid=(n // tm, c // tn),
        in_specs=[pl.BlockSpec((tm, d), lambda i, j: (i, 0)),
                  pl.BlockSpec((1, d), lambda i, j: (0, 0)),
                  pl.BlockSpec((d, tn), lambda i, j: (0, j))],
        out_specs=pl.BlockSpec((tm, tn), lambda i, j: (i, j)),
        out_shape=jax.ShapeDtypeStruct((n, c), out_dtype),
        scratch_shapes=[pltpu.VMEM((tm, d), BF16)],
        name="norm_matmul",
        compiler_params=_cparams(("parallel", "arbitrary")),
    )(x, g, w)


NEG_BIG = -1e30
ATTN_Q_SCALE = DIFF_HEAD_DIM ** -0.5 * math.log2(math.e)


def _diff_attn_body(tq, lam_init, lp_ref, g_ref, q_ref, kt_ref, v_ref, o_ref):
    qi = pl.program_id(2)
    d = DIFF_HEAD_DIM
    lp = lp_ref[...]
    lam = (jnp.exp(jnp.sum(lp[0:1] * lp[1:2], axis=-1, keepdims=True))
           - jnp.exp(jnp.sum(lp[2:3] * lp[3:4], axis=-1, keepdims=True)) + lam_init)
    q = q_ref[...]
    lane = lax.broadcasted_iota(jnp.int32, q.shape, 1)
    q_maps = (jnp.where(lane < d, q, jnp.zeros_like(q)), jnp.where(lane >= d, q, jnp.zeros_like(q)))

    def step(j, carry, masked):
        start = pl.multiple_of(j * tq, tq)
        kt = kt_ref[:, pl.ds(start, tq)]
        vb = v_ref[pl.ds(start, tq), :]
        new = []
        for c in range(2):
            m, l, acc = carry[c]
            s = jnp.dot(q_maps[c], kt, preferred_element_type=F32)
            if masked:
                row = lax.broadcasted_iota(jnp.int32, s.shape, 0)
                col = lax.broadcasted_iota(jnp.int32, s.shape, 1)
                s = jnp.where(col <= row, s, NEG_BIG)
            m_new = jnp.maximum(m, jnp.max(s, axis=-1, keepdims=True))
            alpha = jnp.exp2(m - m_new)
            pr = jnp.exp2(s - m_new)
            l_new = alpha * l + jnp.sum(pr, axis=-1, keepdims=True)
            acc_new = alpha * acc + jnp.dot(pr.astype(BF16), vb, preferred_element_type=F32)
            new.append((m_new, l_new, acc_new))
        return tuple(new)

    init_one = (jnp.full((tq, 1), NEG_BIG, F32), jnp.zeros((tq, 1), F32), jnp.zeros((tq, LANES), F32))
    carry = lax.fori_loop(0, qi, lambda j, c: step(j, c, False), (init_one, init_one))
    (_, l1, acc1), (_, l2, acc2) = step(qi, carry, True)
    o = acc1 / l1 - lam * (acc2 / l2)
    o = o * lax.rsqrt(jnp.mean(o * o, axis=-1, keepdims=True) + SUBLN_EPS)
    o_ref[...] = (o * g_ref[...] * (1.0 - lam_init)).astype(o_ref.dtype)


def diff_attention(qkv, kt, lam_params, subln_g, lam_init, *, tq):
    bsz, seq, c3 = qkv.shape
    n_heads = c3 // (3 * LANES)
    return pl.pallas_call(
        functools.partial(_diff_attn_body, tq, lam_init),
        grid=(bsz, n_heads, seq // tq),
        in_specs=[pl.BlockSpec(lam_params.shape, lambda b, h, i: (0, 0)),
                  pl.BlockSpec((1, LANES), lambda b, h, i: (0, 0)),
                  pl.BlockSpec((None, tq, LANES), lambda b, h, i: (b, i, h)),
                  pl.BlockSpec((None, None, LANES, seq), lambda b, h, i: (b, h, 0, 0)),
                  pl.BlockSpec((None, seq, LANES), lambda b, h, i: (b, 0, 2 * n_heads + h))],
        out_specs=pl.BlockSpec((None, tq, LANES), lambda b, h, i: (b, i, h)),
        out_shape=jax.ShapeDtypeStruct((bsz, seq, n_heads * LANES), BF16),
        name="diff_attention",
        compiler_params=_cparams(("parallel", "parallel", "arbitrary")),
    )(lam_params, subln_g, qkv, kt, qkv)


def _out_proj_body(ya_ref, yb_ref, wa_ref, wb_ref, h_ref, o_ref):
    acc = jnp.dot(ya_ref[...], wa_ref[...], preferred_element_type=F32)
    acc = acc + jnp.dot(yb_ref[...], wb_ref[...], preferred_element_type=F32)
    o_ref[...] = h_ref[...] + acc


def out_proj(ya, yb, wa, wb, h, *, tm, tn):
    n, d = h.shape
    ca, cb = ya.shape[1], yb.shape[1]
    return pl.pallas_call(
        _out_proj_body,
        grid=(n // tm, d // tn),
        in_specs=[pl.BlockSpec((tm, ca), lambda i, j: (i, 0)),
                  pl.BlockSpec((tm, cb), lambda i, j: (i, 0)),
                  pl.BlockSpec((ca, tn), lambda i, j: (0, j)),
                  pl.BlockSpec((cb, tn), lambda i, j: (0, j)),
                  pl.BlockSpec((tm, tn), lambda i, j: (i, j))],
        out_specs=pl.BlockSpec((tm, tn), lambda i, j: (i, j)),
        out_shape=jax.ShapeDtypeStruct((n, d), F32),
        name="out_proj",
        compiler_params=_cparams(("parallel", "parallel")),
    )(ya, yb, wa, wb, h)


def _swiglu_tile(u, w1_ref, w3_ref, w2_ref):
    a = jnp.dot(u, w1_ref[...], preferred_element_type=F32)
    b = jnp.dot(u, w3_ref[...], preferred_element_type=F32)
    mid = (a * _sigmoid(a) * b).astype(BF16)
    return jnp.dot(mid, w2_ref[...], preferred_element_type=F32)


def _dense_ffn_body(x_ref, g_ref, w1_ref, w3_ref, w2_ref, o_ref, u_s, acc_s):
    f = pl.program_id(1)

    @pl.when(f == 0)
    def _():
        u_s[...] = _rmsnorm_rows(x_ref[...], g_ref[...]).astype(BF16)
        acc_s[...] = jnp.zeros_like(acc_s)

    acc_s[...] += _swiglu_tile(u_s[...], w1_ref, w3_ref, w2_ref)

    @pl.when(f == pl.num_programs(1) - 1)
    def _():
        o_ref[...] = x_ref[...] + acc_s[...]


def dense_ffn(x, g, w1, w3, w2, *, tm, tf):
    n, d = x.shape
    ff = w1.shape[1]
    return pl.pallas_call(
        _dense_ffn_body,
        grid=(n // tm, ff // tf),
        in_specs=[pl.BlockSpec((tm, d), lambda i, f: (i, 0)),
                  pl.BlockSpec((1, d), lambda i, f: (0, 0)),
                  pl.BlockSpec((d, tf), lambda i, f: (0, f)),
                  pl.BlockSpec((d, tf), lambda i, f: (0, f)),
                  pl.BlockSpec((tf, d), lambda i, f: (f, 0))],
        out_specs=pl.BlockSpec((tm, d), lambda i, f: (i, 0)),
        out_shape=jax.ShapeDtypeStruct((n, d), F32),
        scratch_shapes=[pltpu.VMEM((tm, d), BF16), pltpu.VMEM((tm, d), F32)],
        name="dense_ffn",
        compiler_params=_cparams(("parallel", "arbitrary")),
    )(x, g, w1, w3, w2)


ROUTE_ROWS = 8


def _router_body(n_experts, x_ref, g_ref, wr_ref, u_ref, route_ref):
    u = _rmsnorm_rows(x_ref[...], g_ref[...])
    u_ref[...] = u.astype(BF16)
    u_hi, u_lo = _split2(u)
    wr = wr_ref[...]
    w_hi, w_lo = _split2(wr)
    logits = (jnp.dot(u_hi, w_hi, preferred_element_type=F32) + jnp.dot(u_hi, w_lo, preferred_element_type=F32)
              + jnp.dot(u_lo, w_hi, preferred_element_type=F32))
    lane = lax.broadcasted_iota(jnp.int32, logits.shape, 1)
    logits = jnp.where(lane < n_experts, logits, NEG_BIG)
    top1 = jnp.max(logits, axis=-1, keepdims=True)
    idx1 = jnp.min(jnp.where(logits == top1, lane, LANES), axis=-1, keepdims=True)
    rest = jnp.where(lane == idx1, NEG_BIG, logits)
    top2 = jnp.max(rest, axis=-1, keepdims=True)
    idx2 = jnp.min(jnp.where(rest == top2, lane, LANES), axis=-1, keepdims=True)
    e2 = jnp.exp(top2 - top1)
    gate1 = 1.0 / (1.0 + e2)
    slab = jnp.where(lane == 0, idx1.astype(F32),
                     jnp.where(lane == 1, idx2.astype(F32),
                               jnp.where(lane == 2, gate1, jnp.where(lane == 3, e2 * gate1, 0.0))))
    route_ref[...] = slab.T[:ROUTE_ROWS, :]


def moe_router(x, g, router_w, *, tm):
    n, d = x.shape
    n_experts = router_w.shape[1]
    wr = jnp.zeros((d, LANES), F32).at[:, :n_experts].set(router_w)
    return pl.pallas_call(
        functools.partial(_router_body, n_experts),
        grid=(n // tm,),
        in_specs=[pl.BlockSpec((tm, d), lambda i: (i, 0)),
                  pl.BlockSpec((1, d), lambda i: (0, 0)),
                  pl.BlockSpec((d, LANES), lambda i: (0, 0))],
        out_specs=(pl.BlockSpec((tm, d), lambda i: (i, 0)),
                   pl.BlockSpec((ROUTE_ROWS, tm), lambda i: (0, i))),
        out_shape=(jax.ShapeDtypeStruct((n, d), BF16),
                   jax.ShapeDtypeStruct((ROUTE_ROWS, n), F32)),
        name="moe_router",
        compiler_params=_cparams(("parallel",)),
    )(x, g, wr)


def _moe_ffn_body(be_ref, nu_ref, x_ref, gate_ref, w1_ref, w3_ref, w2_ref, o_ref, acc_s):
    i, f = pl.program_id(0), pl.program_id(1)
    used = i < nu_ref[0]

    @pl.when(jnp.logical_and(used, f == 0))
    def _():
        acc_s[...] = jnp.zeros_like(acc_s)

    @pl.when(used)
    def _():
        acc_s[...] += _swiglu_tile(x_ref[...], w1_ref, w3_ref, w2_ref)

    @pl.when(f == pl.num_programs(1) - 1)
    def _():
        o_ref[...] = jnp.where(used, acc_s[...] * gate_ref[...], 0.0)


def moe_ffn(x_rows, row_gate, blk_expert, n_used, w1, w3, w2, layer, *, tm, tf):
    rows, d = x_rows.shape
    ff = w1.shape[3]
    nf = ff // tf

    def f_idx(i, f, nu):
        return jnp.where(i < nu[0], f, nf - 1)

    grid_spec = pltpu.PrefetchScalarGridSpec(
        num_scalar_prefetch=2,
        grid=(rows // tm, nf),
        in_specs=[pl.BlockSpec((tm, d), lambda i, f, be, nu: (i, 0)),
                  pl.BlockSpec((tm, 1), lambda i, f, be, nu: (i, 0)),
                  pl.BlockSpec((None, None, d, tf), lambda i, f, be, nu: (layer, be[i], 0, f_idx(i, f, nu))),
                  pl.BlockSpec((None, None, d, tf), lambda i, f, be, nu: (layer, be[i], 0, f_idx(i, f, nu))),
                  pl.BlockSpec((None, None, tf, d), lambda i, f, be, nu: (layer, be[i], f_idx(i, f, nu), 0))],
        out_specs=pl.BlockSpec((tm, d), lambda i, f, be, nu: (i, 0)),
        scratch_shapes=[pltpu.VMEM((tm, d), F32)],
    )
    return pl.pallas_call(
        _moe_ffn_body, grid_spec=grid_spec,
        out_shape=jax.ShapeDtypeStruct((rows, d), F32),
        name="moe_ffn",
        compiler_params=_cparams(("arbitrary", "arbitrary")),
    )(blk_expert, n_used, x_rows, row_gate, w1, w3, w2)


def moe_layer(h, g, router_w, w1, w3, w2, layer, *, tm_router, tm, tf):
    n, d = h.shape
    n_experts = router_w.shape[1]
    u, route = moe_router(h, g, router_w, tm=tm_router)
    e_flat = route[:TOP_K].T.reshape(-1).astype(jnp.int32)
    g_flat = route[TOP_K:2 * TOP_K].T.reshape(-1)
    m = n * TOP_K
    onehot = (e_flat[:, None] == jnp.arange(n_experts, dtype=jnp.int32)[None, :]).astype(jnp.int32)
    counts = jnp.sum(onehot, axis=0)
    rank = jnp.sum((jnp.cumsum(onehot, axis=0) - onehot) * onehot, axis=1)
    padded = (counts + tm - 1) // tm * tm
    pad_ends = jnp.cumsum(padded)
    dest = (pad_ends - padded)[e_flat] + rank
    n_blocks = m // tm + n_experts
    rows = n_blocks * tm
    row_slot = jnp.full((rows,), -1, jnp.int32).at[dest].set(jnp.arange(m, dtype=jnp.int32))
    row_tok = jnp.maximum(row_slot, 0) // TOP_K
    row_gate = jnp.where(row_slot >= 0, g_flat[jnp.maximum(row_slot, 0)], 0.0)
    blk_start = jnp.arange(n_blocks, dtype=jnp.int32) * tm
    blk_expert = jnp.minimum(jnp.sum(blk_start[:, None] >= pad_ends[None, :], axis=1), n_experts - 1).astype(jnp.int32)
    n_used = (pad_ends[-1:] // tm).astype(jnp.int32)
    y_rows = moe_ffn(u[row_tok], row_gate[:, None], blk_expert, n_used, w1, w3, w2, layer, tm=tm, tf=tf)
    dest = dest.reshape(n, TOP_K)
    return [y_rows[dest[:, k]] for k in range(TOP_K)]


def _ple_body(final_norm, n_add, x_ref, *refs):
    g_ref, wg_ref, p_ref, wp_ref, gf_ref, o_ref = refs[n_add:]
    x = x_ref[...]
    for add_ref in refs[:n_add]:
        x = x + add_ref[...]
    u = _rmsnorm_rows(x, g_ref[...]).astype(BF16)
    gate = _sigmoid(jnp.dot(u, wg_ref[...], preferred_element_type=F32))
    proj = jnp.dot(p_ref[...].astype(BF16), wp_ref[...], preferred_element_type=F32)
    out = x + gate * proj
    if final_norm:
        out = _rmsnorm_rows(out, gf_ref[...])
    o_ref[...] = out


def ple_layer(x, addends, g, w_gate, p, w_proj, g_final, final_norm, *, tm):
    n, d = x.shape
    dp = p.shape[1]
    add_args = list(addends)
    add_specs = [pl.BlockSpec((tm, d), lambda i: (i, 0)) for _ in add_args]
    return pl.pallas_call(
        functools.partial(_ple_body, final_norm, len(add_args)),
        grid=(n // tm,),
        in_specs=[pl.BlockSpec((tm, d), lambda i: (i, 0))] + add_specs + [
                  pl.BlockSpec((1, d), lambda i: (0, 0)),
                  pl.BlockSpec((d, d), lambda i: (0, 0)),
                  pl.BlockSpec((tm, dp), lambda i: (i, 0)),
                  pl.BlockSpec((dp, d), lambda i: (0, 0)),
                  pl.BlockSpec((1, d), lambda i: (0, 0))],
        out_specs=pl.BlockSpec((tm, d), lambda i: (i, 0)),
        out_shape=jax.ShapeDtypeStruct((n, d), F32),
        name="ple_layer",
        compiler_params=_cparams(("parallel",)),
    )(x, *add_args, g, w_gate, p, w_proj, g_final)


def _lambda_init(layer_idx):
    return 0.8 - 0.6 * math.exp(-0.3 * layer_idx)


def _tile(n, pref):
    t = min(n, pref)
    while n % t:
        t //= 2
    return t


def kernel(x, p, g_mix, w_in, w_vres_down, mu_shift, mu_vres, rw_w0, rw_w2, rw_a0, rw_a2, rw_g2, rw_v0, rw_v2, rw_k_k, rw_k_a, rw_r_k, rw_lnx_w, rw_lnx_b, diff_lambda, diff_subln, w_out, g_ffn, dense_w1, dense_w3, dense_w2, router_w, moe_w1, moe_w3, moe_w2, g_ple, w_ple_gate, w_ple_proj, g_final):
    bsz, seq, d = x.shape
    depth = w_in.shape[0]
    n = bsz * seq
    c_rwkv = rw_w0.shape[1]
    shift_cols = mu_shift.shape[1]
    zr_cols = 3 * c_rwkv + LORA_COLS
    row = lambda v: v.reshape(1, -1)
    tm = _tile(n, 1024)
    h = x.reshape(n, d)
    moe_w = (moe_w1.astype(BF16), moe_w3.astype(BF16), moe_w2.astype(BF16))
    v_first = None
    for i in range(depth):
        w_vres = w_vres_down[i - 1] if i > 0 else jnp.zeros((d, D_MV_LORA), F32)
        w_r = jnp.concatenate([w_in[i][:, :shift_cols], w_vres,
                               jnp.zeros((d, zr_cols - shift_cols - D_MV_LORA), F32)], axis=1).astype(BF16)
        w_d = w_in[i][:, shift_cols:]
        c_diff = w_d.shape[1] // 3
        w_d = jnp.concatenate([w_d[:, :c_diff] * ATTN_Q_SCALE, w_d[:, c_diff:]], axis=1).astype(BF16)
        z_r = norm_matmul(h, row(g_mix[i]), w_r, F32, tm=tm, tn=zr_cols // 3)
        qkv = norm_matmul(h, row(g_mix[i]), w_d, BF16, tm=tm, tn=w_d.shape[1] // 3)
        packed = _rwkv_params(i, mu_shift, mu_vres, rw_w0, rw_w2, rw_a0, rw_a2, rw_g2, rw_v0, rw_v2,
                              rw_k_k, rw_k_a, rw_r_k, rw_lnx_w, rw_lnx_b)
        mixed = rwkv_time_mix(z_r.reshape(bsz, seq, zr_cols), *packed, v_first, tb=_tile(seq, 512), nt=4)
        if i == 0:
            y_rwkv, v_first = mixed
        else:
            y_rwkv = mixed
        qkv = qkv.reshape(bsz, seq, -1)
        c_diff = qkv.shape[-1] // 3
        kt = qkv[:, :, c_diff:2 * c_diff].reshape(bsz, seq, c_diff // LANES, LANES).transpose(0, 2, 3, 1)
        y_diff = diff_attention(qkv, kt, diff_lambda[i], row(diff_subln[i]), _lambda_init(i), tq=_tile(seq, 1024))
        wo = w_out[i].astype(BF16)
        h = out_proj(y_rwkv.reshape(n, c_rwkv), y_diff.reshape(n, -1), wo[:c_rwkv], wo[c_rwkv:], h,
                     tm=tm, tn=_tile(d, 1024))
        expert_out = []
        if i % 2 == 0:
            j = i // 2
            h = dense_ffn(h, row(g_ffn[i]), dense_w1[j].astype(BF16), dense_w3[j].astype(BF16),
                          dense_w2[j].astype(BF16), tm=_tile(n, 512), tf=512)
        else:
            expert_out = moe_layer(h, row(g_ffn[i]), router_w[i // 2], *moe_w, i // 2,
                                   tm_router=_tile(n, 512), tm=_tile(n, 1024), tf=512)
        h = ple_layer(h, expert_out, row(g_ple[i]), w_ple_gate[i].astype(BF16), p[i].reshape(n, -1),
                      w_ple_proj[i].astype(BF16), row(g_final), i == depth - 1,
                      tm=_tile(n, 256 if expert_out else 512))
    return h.reshape(bsz, seq, d)
```

```python
import functools
import math

import jax
import jax.numpy as jnp
from jax import lax
from jax.experimental import pallas as pl
from jax.experimental.pallas import tpu as pltpu

F32 = jnp.float32
BF16 = jnp.bfloat16

LANES = 128
VMEM_LIMIT = 56 * 1024 * 1024

RWKV_HEAD = 64
HEADS_PER_TILE = LANES // RWKV_HEAD
DIFF_HEAD_DIM = 64
D_DECAY_LORA, D_AAA_LORA, D_MV_LORA, D_GATE_LORA = 64, 64, 32, 160
LORA_COLS = 3 * LANES
TOP_K = 2
RMS_EPS = 1e-6
GN_EPS = 64e-5
SUBLN_EPS = 1e-5

WKV_CHUNK = 64


def _cparams(sem):
    return pltpu.CompilerParams(dimension_semantics=sem, vmem_limit_bytes=VMEM_LIMIT)


def _dot01_right(x, m01, nsplit=3):
    acc, rem = None, x
    for _ in range(nsplit):
        piece = rem.astype(BF16)
        t = jnp.dot(piece, m01, preferred_element_type=F32)
        acc = t if acc is None else acc + t
        rem = rem - piece.astype(F32)
    return acc


def _split2(x):
    hi = x.astype(BF16)
    return hi, (x - hi.astype(F32)).astype(BF16)


def _sdot_tn(a, b):
    return lax.dot_general(a.astype(BF16), b.astype(BF16), (((0,), (0,)), ((), ())), preferred_element_type=F32)


def _bmm(a, b):
    return lax.dot_general(a.astype(BF16), b.astype(BF16), (((2,), (1,)), ((0,), (0,))), preferred_element_type=F32)


def _bmm_nt(a, b):
    return lax.dot_general(a.astype(BF16), b.astype(BF16), (((2,), (2,)), ((0,), (0,))), preferred_element_type=F32)


def _bdot01(m01, x, nsplit):
    acc, rem = None, x
    for _ in range(nsplit):
        piece = rem.astype(BF16)
        t = lax.dot_general(m01, piece, (((2,), (1,)), ((0,), (0,))), preferred_element_type=F32)
        acc = t if acc is None else acc + t
        rem = rem - piece.astype(F32)
    return acc


def _dot3(a, b):
    dn = (((1,), (0,)), ((), ()))
    a_hi, a_lo = _split2(a)
    b_hi, b_lo = _split2(b)
    out = lax.dot_general(a_hi, b_hi, dn, preferred_element_type=F32)
    out = out + lax.dot_general(a_hi, b_lo, dn, preferred_element_type=F32)
    return out + lax.dot_general(a_lo, b_hi, dn, preferred_element_type=F32)


def _token_shift(x, prev_row, mu):
    rolled = pltpu.roll(x, 1, 0)
    row = lax.broadcasted_iota(jnp.int32, x.shape, 0)
    prev = jnp.where(row == 0, prev_row, rolled)
    return x + mu * (prev - x)


def _sigmoid(x):
    return 1.0 / (1.0 + jnp.exp(-x))


def _wkv_body(first_layer, tb, nt, *refs):
    if first_layer:
        (zr, zk, zv, zl, par, mul, w2, a2, g2, v2,
         y_out, vf_out, p_s, prev_s, prevl_s, st_s, y_s, wk_s, mn_s) = refs
        vf_in = None
    else:
        (zr, zk, zv, zl, par, mul, w2, a2, g2, v2, vf_in,
         y_out, p_s, prev_s, prevl_s, st_s, y_s, wk_s, mn_s) = refs
    L = WKV_CHUNK
    n_chunks = tb // L

    @pl.when(pl.program_id(2) == 0)
    def _():
        p_s[...] = jnp.zeros_like(p_s)
        prev_s[...] = jnp.zeros_like(prev_s)
        prevl_s[...] = jnp.zeros_like(prevl_s)

    lane = lax.broadcasted_iota(jnp.int32, (LANES, LANES), 1)
    sub = lax.broadcasted_iota(jnp.int32, (LANES, LANES), 0)
    same_head = (lane // RWKV_HEAD) == (sub // RWKV_HEAD)
    block_ones = jnp.where(same_head, 1.0, 0.0).astype(BF16)

    zl_v = zl[...]
    ls = _token_shift(zl_v, prevl_s[...], mul[...])
    prevl_s[...] = zl_v[tb - 1:tb, :]
    l0 = ls[:, 0:LANES]
    l0_tanh = jnp.tanh(l0).astype(BF16)
    l0 = l0.astype(BF16)
    l1_sig = _sigmoid(ls[:, LANES:3 * LANES]).astype(BF16)
    l2 = ls[:, 2 * LANES:3 * LANES].astype(BF16)
    for u in range(nt):
        cols = slice(u * LANES, (u + 1) * LANES)
        rows = slice(u * tb, (u + 1) * tb)
        zr_v, zk_v, zv_v = zr[:, cols], zk[:, cols], zv[:, cols]
        r = _token_shift(zr_v, prev_s[u, 0:1, :], par[0:1, cols])
        k = _token_shift(zk_v, prev_s[u, 1:2, :], par[1:2, cols])
        v = _token_shift(zv_v, prev_s[u, 2:3, :], par[2:3, cols])
        prev_s[u, 0:1, :] = zr_v[tb - 1:tb, :]
        prev_s[u, 1:2, :] = zk_v[tb - 1:tb, :]
        prev_s[u, 2:3, :] = zv_v[tb - 1:tb, :]
        wl = par[3:4, cols] + jnp.dot(l0_tanh, w2[:, cols], preferred_element_type=F32)
        al = par[4:5, cols] + jnp.dot(l0, a2[:, cols], preferred_element_type=F32)
        gate = jnp.dot(l1_sig, g2[:, cols], preferred_element_type=F32)
        if first_layer:
            vf_out[:, cols] = v
        else:
            vg = _sigmoid(par[5:6, cols] + jnp.dot(l2, v2[:, cols], preferred_element_type=F32))
            v = v + (vf_in[:, cols] - v) * vg
        neg_wl = -wl
        softplus = jnp.maximum(neg_wl, 0.0) + jnp.log(1.0 + jnp.exp(-jnp.abs(neg_wl)))
        logw = -jnp.exp(-softplus - 0.5)
        asig = _sigmoid(al)
        kk = k * par[6:7, cols]
        kk_norm = jnp.sqrt(_dot01_right(kk * kk, block_ones))
        kk = kk / jnp.maximum(kk_norm, 1e-12)
        k2 = k * (1.0 + (asig - 1.0) * par[7:8, cols])
        bonus = _dot01_right(r * k2 * par[8:9, cols], block_ones) * v
        st_s[0, rows, :] = r
        st_s[1, rows, :] = logw
        st_s[2, rows, :] = k2
        st_s[3, rows, :] = v
        st_s[4, rows, :] = -kk
        st_s[5, rows, :] = kk * asig
        st_s[6, rows, :] = gate
        st_s[7, rows, :] = bonus

    C = nt * n_chunks
    tb_all = nt * tb
    ti = lax.broadcasted_iota(jnp.int32, (C, L, L), 1)
    si = lax.broadcasted_iota(jnp.int32, (C, L, L), 2)
    tri01 = jnp.where(si <= ti, 1.0, 0.0).astype(BF16)
    ti2 = lax.broadcasted_iota(jnp.int32, (2 * C, 2 * L, L), 1)
    si2 = lax.broadcasted_iota(jnp.int32, (2 * C, 2 * L, L), 2)
    g_mask = si2 <= jnp.where(ti2 < L, ti2 - 1, ti2 - L)
    eye = jnp.where(lax.broadcasted_iota(jnp.int32, (2 * C, L, L), 1)
                    == lax.broadcasted_iota(jnp.int32, (2 * C, L, L), 2), 1.0, 0.0).astype(F32)
    head0 = lax.broadcasted_iota(jnp.int32, (C, L, LANES), 2) < RWKV_HEAD

    chunks = lambda x: x.reshape(C, L, LANES)
    r3, lw3, k3, v3, a3, b3 = (chunks(st_s[i]) for i in range(6))
    cum = _bdot01(tri01, lw3, nsplit=2)
    c_last = cum[:, L - 1:L, :]
    e_inv = jnp.exp(-cum)
    e_end = jnp.exp(c_last - cum)
    a_t = a3 * jnp.exp(cum - lw3)
    b_t = b3 * e_inv
    k_t = k3 * e_inv
    r_t = r3 * jnp.exp(cum)
    two = lambda x: jnp.concatenate([x, x], axis=0)
    per_head = lambda x: jnp.concatenate([jnp.where(head0, x, 0.0), jnp.where(head0, 0.0, x)], axis=0)
    lhs = jnp.concatenate([per_head(a_t), per_head(r_t)], axis=1).astype(BF16)
    g_b = jnp.where(g_mask, _bmm_nt(lhs, two(b_t)), 0.0)
    g_k = jnp.where(g_mask, _bmm_nt(lhs, two(k_t)), 0.0)
    a_ab, a_rb = g_b[:, :L], g_b[:, L:]
    a_ak, a_rk = g_k[:, :L], g_k[:, L:]
    t_inv = eye + a_ab
    pw = _bmm(a_ab, a_ab)
    steps = int(math.log2(L)) - 1
    for j in range(steps):
        if j + 1 < steps:
            both = _bmm(jnp.concatenate([t_inv, pw], axis=1), pw)
            t_inv = t_inv + both[:, :L]
            pw = both[:, L:]
        else:
            t_inv = t_inv + _bmm(t_inv, pw)
    v2x = two(v3)
    ta_h = _bmm(t_inv, two(a_t))
    u0_h = _bmm(t_inv, _bmm(a_ak, v2x))
    rp_h = _bmm(a_rb, ta_h)
    y1_h = _bmm(a_rb, u0_h) + _bmm(a_rk, v2x)
    merge = lambda x: jnp.where(head0, x[:C], x[C:])
    flat = lambda x: x.reshape(tb_all, LANES)
    wk_s[0] = flat(merge(ta_h))
    wk_s[1] = flat(merge(u0_h))
    wk_s[2] = flat(r_t + merge(rp_h))
    wk_s[3] = flat(merge(y1_h))
    wk_s[4] = flat(b3 * e_end)
    wk_s[5] = flat(k3 * e_end)
    wk_s[6] = flat(cum)

    diag = lane == sub

    def chunk_maps(c, carry):
        sl = pl.ds(pl.multiple_of(c * L, L), L)
        ta_c, u0_c, bh_c, kh_c = wk_s[0, sl, :], wk_s[1, sl, :], wk_s[4, sl, :], wk_s[5, sl, :]
        tail = wk_s[6, pl.ds(pl.multiple_of(c * L + L - 8, 8), 8), :]
        w_end = jnp.exp(tail[7:8, :])
        m = _sdot_tn(bh_c, ta_c) + jnp.where(diag, w_end, 0.0)
        n = _sdot_tn(bh_c, u0_c) + _sdot_tn(kh_c, st_s[3, sl, :])
        mn_s[c, :, 0:LANES] = jnp.where(same_head, m, 0.0)
        mn_s[c, :, LANES:2 * LANES] = jnp.where(same_head, n, 0.0)
        return carry

    lax.fori_loop(0, C, chunk_maps, 0, unroll=4)

    def chain(c, states):
        new = []
        for u in range(nt):
            cu = c + u * n_chunks
            sl = pl.ds(pl.multiple_of(cu * L, L), L)
            y_s[sl, :] = _dot3(wk_s[2, sl, :], states[u]) + wk_s[3, sl, :]
            new.append(_dot3(mn_s[cu, :, 0:LANES], states[u]) + mn_s[cu, :, LANES:2 * LANES])
        return tuple(new)

    states = lax.fori_loop(0, n_chunks, chain, tuple(p_s[u] for u in range(nt)))
    for u in range(nt):
        p_s[u] = states[u]

    inv_n = 1.0 / RWKV_HEAD
    for u in range(nt):
        cols = slice(u * LANES, (u + 1) * LANES)
        rows = slice(u * tb, (u + 1) * tb)
        y = y_s[rows, :]
        mean = _dot01_right(y, block_ones) * inv_n
        d = y - mean
        var = _dot01_right(d * d, block_ones) * inv_n
        yn = d * lax.rsqrt(var + GN_EPS) * par[9:10, cols] + par[10:11, cols]
        y_out[:, cols] = ((yn + st_s[7, rows, :]) * st_s[6, rows, :]).astype(y_out.dtype)


def rwkv_time_mix(z_r, par, mul, w2p, a2p, g2p, v2p, v_first, *, tb, nt):
    bsz, seq, cols = z_r.shape
    c_rwkv = (cols - LORA_COLS) // 3
    n_groups = c_rwkv // (nt * LANES)
    wide = nt * LANES
    first = v_first is None
    grid = (bsz, n_groups, seq // tb)
    col_spec = lambda off: pl.BlockSpec((None, tb, wide), lambda b, h, t: (b, t, off + h))
    par_rows = par.shape[0]
    in_specs = [
        col_spec(0), col_spec(n_groups), col_spec(2 * n_groups),
        pl.BlockSpec((None, tb, LORA_COLS), lambda b, h, t: (b, t, 3 * c_rwkv // LORA_COLS)),
        pl.BlockSpec((par_rows, wide), lambda b, h, t: (0, h)),
        pl.BlockSpec((1, LORA_COLS), lambda b, h, t: (0, 0)),
        pl.BlockSpec((LANES, wide), lambda b, h, t: (0, h)),
        pl.BlockSpec((LANES, wide), lambda b, h, t: (0, h)),
        pl.BlockSpec((2 * LANES, wide), lambda b, h, t: (0, h)),
        pl.BlockSpec((LANES, wide), lambda b, h, t: (0, h)),
    ]
    args = [z_r, z_r, z_r, z_r, par, mul, w2p, a2p, g2p, v2p]
    y_spec = pl.BlockSpec((None, tb, wide), lambda b, h, t: (b, t, h))
    y_shape = jax.ShapeDtypeStruct((bsz, seq, c_rwkv), BF16)
    if first:
        out_shape = (y_shape, jax.ShapeDtypeStruct((bsz, seq, c_rwkv), F32))
        out_specs = (y_spec, y_spec)
    else:
        in_specs.append(y_spec)
        args.append(v_first)
        out_shape = y_shape
        out_specs = y_spec
    scratch = [
        pltpu.VMEM((nt, LANES, LANES), F32),
        pltpu.VMEM((nt, 8, LANES), F32),
        pltpu.VMEM((1, LORA_COLS), F32),
        pltpu.VMEM((8, nt * tb, LANES), F32),
        pltpu.VMEM((nt * tb, LANES), F32),
        pltpu.VMEM((7, nt * tb, LANES), F32),
        pltpu.VMEM((nt * tb // WKV_CHUNK, LANES, 2 * LANES), F32),
    ]
    return pl.pallas_call(
        functools.partial(_wkv_body, first, tb, nt),
        grid=grid, in_specs=in_specs, out_specs=out_specs, out_shape=out_shape,
        scratch_shapes=scratch, name="rwkv7_time_mix",
        compiler_params=_cparams(("parallel", "parallel", "arbitrary")),
    )(*args)


def _rwkv_params(i, mu_shift, mu_vres, rw_w0, rw_w2, rw_a0, rw_a2, rw_g2, rw_v0, rw_v2,
                 rw_k_k, rw_k_a, rw_r_k, rw_lnx_w, rw_lnx_b):
    c = rw_w0.shape[1]
    mu = mu_shift[i]
    zero_c = jnp.zeros((c,), F32)
    rows = [mu[:c], mu[c:2 * c], mu[2 * c:3 * c], rw_w0[i], rw_a0[i],
            rw_v0[i - 1] if i > 0 else zero_c, rw_k_k[i], rw_k_a[i], rw_r_k[i].reshape(-1),
            rw_lnx_w[i], rw_lnx_b[i]]
    par = jnp.stack(rows + [zero_c] * (16 - len(rows)))
    n_lora = D_DECAY_LORA + D_AAA_LORA + D_GATE_LORA
    mu_v = mu_vres[i - 1] if i > 0 else jnp.zeros((D_MV_LORA,), F32)
    mul = jnp.concatenate([mu[3 * c:], mu_v, jnp.zeros((LORA_COLS - n_lora - D_MV_LORA,), F32)])[None, :]
    w2p = jnp.zeros((LANES, c), F32).at[:D_DECAY_LORA].set(rw_w2[i])
    a2p = jnp.zeros((LANES, c), F32).at[D_DECAY_LORA:D_DECAY_LORA + D_AAA_LORA].set(rw_a2[i])
    g2p = jnp.zeros((2 * LANES, c), F32).at[:D_GATE_LORA].set(rw_g2[i])
    v2p = jnp.zeros((LANES, c), F32)
    if i > 0:
        lv0 = n_lora - 2 * LANES
        v2p = v2p.at[lv0:lv0 + D_MV_LORA].set(rw_v2[i - 1])
    return par, mul, w2p.astype(BF16), a2p.astype(BF16), g2p.astype(BF16), v2p.astype(BF16)


def _rmsnorm_rows(x, g):
    ms = jnp.mean(x * x, axis=-1, keepdims=True)
    return x * lax.rsqrt(ms + RMS_EPS) * g


def _norm_matmul_body(t_cols, x_ref, g_ref, w_ref, o_ref, *t_ref):
    u = _rmsnorm_rows(x_ref[...], g_ref[...]).astype(BF16)
    z = jnp.dot(u, w_ref[...], preferred_element_type=F32)
    o_ref[...] = z.astype(o_ref.dtype)
    if t_cols is not None:
        t_ref[0][...] = z[:, t_cols[0]:t_cols[1]].T.astype(t_ref[0].dtype)


def norm_matmul(x, g, w, out_dtype, t_cols=None, *, tm):
    n, d = x.shape
    c = w.shape[1]
    out_shape = [jax.ShapeDtypeStruct((n, c), out_dtype)]
    out_specs = [pl.BlockSpec((tm, c), lambda i: (i, 0))]
    if t_cols is not None:
        ct = t_cols[1] - t_cols[0]
        out_shape.append(jax.ShapeDtypeStruct((ct, n), out_dtype))
        out_specs.append(pl.BlockSpec((ct, tm), lambda i: (0, i)))
    return pl.pallas_call(
        functools.partial(_norm_matmul_body, t_cols),
        grid=(n // tm,),
        in_specs=[pl.BlockSpec((tm, d), lambda i: (i, 0)),
                  pl.BlockSpec((1, d), lambda i: (0, 0)),
                  pl.BlockSpec((d, c), lambda i: (0, 0))],
        out_specs=out_specs, out_shape=out_shape,
        name="norm_matmul",
        compiler_params=_cparams(("parallel",)),
    )(x, g, w)


NEG_BIG = -1e30
ATTN_Q_SCALE = DIFF_HEAD_DIM ** -0.5 * math.log2(math.e)


def _diff_attn_body(tq, lam_init, lp_ref, g_ref, q_ref, kt_ref, v_ref, o_ref):
    qi = pl.program_id(2)
    d = DIFF_HEAD_DIM
    lp = lp_ref[...]
    lam = (jnp.exp(jnp.sum(lp[0:1] * lp[1:2], axis=-1, keepdims=True))
           - jnp.exp(jnp.sum(lp[2:3] * lp[3:4], axis=-1, keepdims=True)) + lam_init)
    q = q_ref[...]
    lane = lax.broadcasted_iota(jnp.int32, q.shape, 1)
    q_maps = (jnp.where(lane < d, q, jnp.zeros_like(q)), jnp.where(lane >= d, q, jnp.zeros_like(q)))

    def step(j, carry, masked):
        start = pl.multiple_of(j * tq, tq)
        kt = kt_ref[:, pl.ds(start, tq)]
        vb = v_ref[pl.ds(start, tq), :]
        new = []
        for c in range(2):
            m, l, acc = carry[c]
            s = jnp.dot(q_maps[c], kt, preferred_element_type=F32)
            if masked:
                row = lax.broadcasted_iota(jnp.int32, s.shape, 0)
                col = lax.broadcasted_iota(jnp.int32, s.shape, 1)
                s = jnp.where(col <= row, s, NEG_BIG)
            m_new = jnp.maximum(m, jnp.max(s, axis=-1, keepdims=True))
            alpha = jnp.exp2(m - m_new)
            pr = jnp.exp2(s - m_new)
            l_new = alpha * l + jnp.sum(pr, axis=-1, keepdims=True)
            acc_new = alpha * acc + jnp.dot(pr.astype(BF16), vb, preferred_element_type=F32)
            new.append((m_new, l_new, acc_new))
        return tuple(new)

    init_one = (jnp.full((tq, 1), NEG_BIG, F32), jnp.zeros((tq, 1), F32), jnp.zeros((tq, LANES), F32))
    carry = lax.fori_loop(0, qi, lambda j, c: step(j, c, False), (init_one, init_one))
    (_, l1, acc1), (_, l2, acc2) = step(qi, carry, True)
    o = acc1 / l1 - lam * (acc2 / l2)
    o = o * lax.rsqrt(jnp.mean(o * o, axis=-1, keepdims=True) + SUBLN_EPS)
    o_ref[...] = (o * g_ref[...] * (1.0 - lam_init)).astype(o_ref.dtype)


def diff_attention(qkv, kt, lam_params, subln_g, lam_init, *, tq):
    bsz, seq, c3 = qkv.shape
    n_heads = c3 // (3 * LANES)
    return pl.pallas_call(
        functools.partial(_diff_attn_body, tq, lam_init),
        grid=(bsz, n_heads, seq // tq),
        in_specs=[pl.BlockSpec(lam_params.shape, lambda b, h, i: (0, 0)),
                  pl.BlockSpec((1, LANES), lambda b, h, i: (0, 0)),
                  pl.BlockSpec((None, tq, LANES), lambda b, h, i: (b, i, h)),
                  pl.BlockSpec((LANES, seq), lambda b, h, i: (h, b)),
                  pl.BlockSpec((None, seq, LANES), lambda b, h, i: (b, 0, 2 * n_heads + h))],
        out_specs=pl.BlockSpec((None, tq, LANES), lambda b, h, i: (b, i, h)),
        out_shape=jax.ShapeDtypeStruct((bsz, seq, n_heads * LANES), BF16),
        name="diff_attention",
        compiler_params=_cparams(("parallel", "parallel", "arbitrary")),
    )(lam_params, subln_g, qkv, kt, qkv)


def _out_proj_body(ya_ref, yb_ref, wa_ref, wb_ref, h_ref, o_ref):
    acc = jnp.dot(ya_ref[...], wa_ref[...], preferred_element_type=F32)
    acc = acc + jnp.dot(yb_ref[...], wb_ref[...], preferred_element_type=F32)
    o_ref[...] = h_ref[...] + acc


def out_proj(ya, yb, wa, wb, h, *, tm, tn):
    n, d = h.shape
    ca, cb = ya.shape[1], yb.shape[1]
    return pl.pallas_call(
        _out_proj_body,
        grid=(n // tm, d // tn),
        in_specs=[pl.BlockSpec((tm, ca), lambda i, j: (i, 0)),
                  pl.BlockSpec((tm, cb), lambda i, j: (i, 0)),
                  pl.BlockSpec((ca, tn), lambda i, j: (0, j)),
                  pl.BlockSpec((cb, tn), lambda i, j: (0, j)),
                  pl.BlockSpec((tm, tn), lambda i, j: (i, j))],
        out_specs=pl.BlockSpec((tm, tn), lambda i, j: (i, j)),
        out_shape=jax.ShapeDtypeStruct((n, d), F32),
        name="out_proj",
        compiler_params=_cparams(("parallel", "parallel")),
    )(ya, yb, wa, wb, h)


def _swiglu_tile(u, w1_ref, w3_ref, w2_ref):
    a = jnp.dot(u, w1_ref[...], preferred_element_type=F32)
    b = jnp.dot(u, w3_ref[...], preferred_element_type=F32)
    mid = (a * _sigmoid(a) * b).astype(BF16)
    return jnp.dot(mid, w2_ref[...], preferred_element_type=F32)


def _dense_ffn_body(x_ref, g_ref, w1_ref, w3_ref, w2_ref, o_ref, u_s):
    @pl.when(pl.program_id(1) == 0)
    def _():
        x = x_ref[...]
        u_s[...] = _rmsnorm_rows(x, g_ref[...]).astype(BF16)
        o_ref[...] = x

    o_ref[...] += _swiglu_tile(u_s[...], w1_ref, w3_ref, w2_ref)


def dense_ffn(x, g, w1, w3, w2, *, tm, tf):
    n, d = x.shape
    ff = w1.shape[1]
    assert n % tm == 0 and ff % tf == 0
    return pl.pallas_call(
        _dense_ffn_body,
        grid=(n // tm, ff // tf),
        in_specs=[pl.BlockSpec((tm, d), lambda i, f: (i, 0)),
                  pl.BlockSpec((1, d), lambda i, f: (0, 0)),
                  pl.BlockSpec((d, tf), lambda i, f: (0, f)),
                  pl.BlockSpec((d, tf), lambda i, f: (0, f)),
                  pl.BlockSpec((tf, d), lambda i, f: (f, 0))],
        out_specs=pl.BlockSpec((tm, d), lambda i, f: (i, 0)),
        out_shape=jax.ShapeDtypeStruct((n, d), F32),
        scratch_shapes=[pltpu.VMEM((tm, d), BF16)],
        name="dense_ffn",
        compiler_params=_cparams(("parallel", "arbitrary")),
    )(x, g, w1, w3, w2)


ROUTE_ROWS = 8


def _router_body(n_experts, x_ref, g_ref, wr_ref, u_ref, route_ref):
    u = _rmsnorm_rows(x_ref[...], g_ref[...])
    u_ref[...] = u.astype(BF16)
    u_hi, u_lo = _split2(u)
    wr = wr_ref[...]
    w_hi, w_lo = _split2(wr)
    logits = (jnp.dot(u_hi, w_hi, preferred_element_type=F32) + jnp.dot(u_hi, w_lo, preferred_element_type=F32)
              + jnp.dot(u_lo, w_hi, preferred_element_type=F32))
    lane = lax.broadcasted_iota(jnp.int32, logits.shape, 1)
    logits = jnp.where(lane < n_experts, logits, NEG_BIG)
    top1 = jnp.max(logits, axis=-1, keepdims=True)
    idx1 = jnp.min(jnp.where(logits == top1, lane, LANES), axis=-1, keepdims=True)
    rest = jnp.where(lane == idx1, NEG_BIG, logits)
    top2 = jnp.max(rest, axis=-1, keepdims=True)
    idx2 = jnp.min(jnp.where(rest == top2, lane, LANES), axis=-1, keepdims=True)
    e2 = jnp.exp(top2 - top1)
    gate1 = 1.0 / (1.0 + e2)
    slab = jnp.where(lane == 0, idx1.astype(F32),
                     jnp.where(lane == 1, idx2.astype(F32),
                               jnp.where(lane == 2, gate1, jnp.where(lane == 3, e2 * gate1, 0.0))))
    route_ref[...] = slab.T[:ROUTE_ROWS, :]


def moe_router(x, g, router_w, *, tm):
    n, d = x.shape
    n_experts = router_w.shape[1]
    wr = jnp.zeros((d, LANES), F32).at[:, :n_experts].set(router_w)
    return pl.pallas_call(
        functools.partial(_router_body, n_experts),
        grid=(n // tm,),
        in_specs=[pl.BlockSpec((tm, d), lambda i: (i, 0)),
                  pl.BlockSpec((1, d), lambda i: (0, 0)),
                  pl.BlockSpec((d, LANES), lambda i: (0, 0))],
        out_specs=(pl.BlockSpec((tm, d), lambda i: (i, 0)),
                   pl.BlockSpec((ROUTE_ROWS, tm), lambda i: (0, i))),
        out_shape=(jax.ShapeDtypeStruct((n, d), BF16),
                   jax.ShapeDtypeStruct((ROUTE_ROWS, n), F32)),
        name="moe_router",
        compiler_params=_cparams(("parallel",)),
    )(x, g, wr)


def _moe_ffn_body(be_ref, nu_ref, x_ref, gate_ref, w1_ref, w3_ref, w2_ref, o_ref, acc_s):
    i, f = pl.program_id(0), pl.program_id(1)
    used = i < nu_ref[0]

    @pl.when(jnp.logical_and(used, f == 0))
    def _():
        acc_s[...] = jnp.zeros_like(acc_s)

    @pl.when(used)
    def _():
        acc_s[...] += _swiglu_tile(x_ref[...], w1_ref, w3_ref, w2_ref)

    @pl.when(f == pl.num_programs(1) - 1)
    def _():
        o_ref[...] = jnp.where(used, acc_s[...] * gate_ref[...], 0.0)


def moe_ffn(x_rows, row_gate, blk_expert, n_used, w1, w3, w2, layer, *, tm, tf):
    rows, d = x_rows.shape
    ff = w1.shape[3]
    nf = ff // tf

    def f_idx(i, f, nu):
        return jnp.where(i < nu[0], f, nf - 1)

    grid_spec = pltpu.PrefetchScalarGridSpec(
        num_scalar_prefetch=2,
        grid=(rows // tm, nf),
        in_specs=[pl.BlockSpec((tm, d), lambda i, f, be, nu: (i, 0)),
                  pl.BlockSpec((tm, 1), lambda i, f, be, nu: (i, 0)),
                  pl.BlockSpec((None, None, d, tf), lambda i, f, be, nu: (layer, be[i], 0, f_idx(i, f, nu))),
                  pl.BlockSpec((None, None, d, tf), lambda i, f, be, nu: (layer, be[i], 0, f_idx(i, f, nu))),
                  pl.BlockSpec((None, None, tf, d), lambda i, f, be, nu: (layer, be[i], f_idx(i, f, nu), 0))],
        out_specs=pl.BlockSpec((tm, d), lambda i, f, be, nu: (i, 0)),
        scratch_shapes=[pltpu.VMEM((tm, d), F32)],
    )
    return pl.pallas_call(
        _moe_ffn_body, grid_spec=grid_spec,
        out_shape=jax.ShapeDtypeStruct((rows, d), F32),
        name="moe_ffn",
        compiler_params=_cparams(("arbitrary", "arbitrary")),
    )(blk_expert, n_used, x_rows, row_gate, w1, w3, w2)


def moe_layer(h, g, router_w, w1, w3, w2, layer, *, tm_router, tm, tf):
    n, d = h.shape
    n_experts = router_w.shape[1]
    u, route = moe_router(h, g, router_w, tm=tm_router)
    e_flat = route[:TOP_K].T.reshape(-1).astype(jnp.int32)
    g_flat = route[TOP_K:2 * TOP_K].T.reshape(-1)
    m = n * TOP_K
    onehot = (e_flat[:, None] == jnp.arange(n_experts, dtype=jnp.int32)[None, :]).astype(jnp.int32)
    counts = jnp.sum(onehot, axis=0)
    rank = jnp.sum((jnp.cumsum(onehot, axis=0) - onehot) * onehot, axis=1)
    padded = (counts + tm - 1) // tm * tm
    pad_ends = jnp.cumsum(padded)
    dest = (pad_ends - padded)[e_flat] + rank
    n_blocks = m // tm + n_experts
    rows = n_blocks * tm
    row_slot = jnp.full((rows,), -1, jnp.int32).at[dest].set(jnp.arange(m, dtype=jnp.int32))
    row_tok = jnp.maximum(row_slot, 0) // TOP_K
    row_gate = jnp.where(row_slot >= 0, g_flat[jnp.maximum(row_slot, 0)], 0.0)
    blk_start = jnp.arange(n_blocks, dtype=jnp.int32) * tm
    blk_expert = jnp.minimum(jnp.sum(blk_start[:, None] >= pad_ends[None, :], axis=1), n_experts - 1).astype(jnp.int32)
    n_used = (pad_ends[-1:] // tm).astype(jnp.int32)
    y_rows = moe_ffn(u[row_tok], row_gate[:, None], blk_expert, n_used, w1, w3, w2, layer, tm=tm, tf=tf)
    dest = dest.reshape(n, TOP_K)
    return [y_rows[dest[:, k]] for k in range(TOP_K)]


def _ple_body(final_norm, n_add, x_ref, *refs):
    g_ref, wg_ref, p_ref, wp_ref, gf_ref, o_ref = refs[n_add:]
    x = x_ref[...]
    for add_ref in refs[:n_add]:
        x = x + add_ref[...]
    u = _rmsnorm_rows(x, g_ref[...]).astype(BF16)
    gate = _sigmoid(jnp.dot(u, wg_ref[...], preferred_element_type=F32))
    proj = jnp.dot(p_ref[...].astype(BF16), wp_ref[...], preferred_element_type=F32)
    out = x + gate * proj
    if final_norm:
        out = _rmsnorm_rows(out, gf_ref[...])
    o_ref[...] = out


def ple_layer(x, addends, g, w_gate, p, w_proj, g_final, final_norm, *, tm):
    n, d = x.shape
    dp = p.shape[1]
    add_args = list(addends)
    add_specs = [pl.BlockSpec((tm, d), lambda i: (i, 0)) for _ in add_args]
    return pl.pallas_call(
        functools.partial(_ple_body, final_norm, len(add_args)),
        grid=(n // tm,),
        in_specs=[pl.BlockSpec((tm, d), lambda i: (i, 0))] + add_specs + [
                  pl.BlockSpec((1, d), lambda i: (0, 0)),
                  pl.BlockSpec((d, d), lambda i: (0, 0)),
                  pl.BlockSpec((tm, dp), lambda i: (i, 0)),
                  pl.BlockSpec((dp, d), lambda i: (0, 0)),
                  pl.BlockSpec((1, d), lambda i: (0, 0))],
        out_specs=pl.BlockSpec((tm, d), lambda i: (i, 0)),
        out_shape=jax.ShapeDtypeStruct((n, d), F32),
        name="ple_layer",
        compiler_params=_cparams(("parallel",)),
    )(x, *add_args, g, w_gate, p, w_proj, g_final)


def _lambda_init(layer_idx):
    return 0.8 - 0.6 * math.exp(-0.3 * layer_idx)


def _tile(n, pref):
    t = min(n, pref)
    while n % t:
        t //= 2
    return t


def kernel(x, p, g_mix, w_in, w_vres_down, mu_shift, mu_vres, rw_w0, rw_w2, rw_a0, rw_a2, rw_g2, rw_v0, rw_v2, rw_k_k, rw_k_a, rw_r_k, rw_lnx_w, rw_lnx_b, diff_lambda, diff_subln, w_out, g_ffn, dense_w1, dense_w3, dense_w2, router_w, moe_w1, moe_w3, moe_w2, g_ple, w_ple_gate, w_ple_proj, g_final):
    bsz, seq, d = x.shape
    depth = w_in.shape[0]
    n = bsz * seq
    c_rwkv = rw_w0.shape[1]
    shift_cols = mu_shift.shape[1]
    zr_cols = 3 * c_rwkv + LORA_COLS
    row = lambda v: v.reshape(1, -1)
    tm = _tile(n, 1024)
    h = x.reshape(n, d)
    moe_w = (moe_w1.astype(BF16), moe_w3.astype(BF16), moe_w2.astype(BF16))
    v_first = None
    for i in range(depth):
        w_vres = w_vres_down[i - 1] if i > 0 else jnp.zeros((d, D_MV_LORA), F32)
        w_r = jnp.concatenate([w_in[i][:, :shift_cols], w_vres,
                               jnp.zeros((d, zr_cols - shift_cols - D_MV_LORA), F32)], axis=1).astype(BF16)
        w_d = w_in[i][:, shift_cols:]
        c_diff = w_d.shape[1] // 3
        w_d = jnp.concatenate([w_d[:, :c_diff] * ATTN_Q_SCALE, w_d[:, c_diff:]], axis=1).astype(BF16)
        z_r, = norm_matmul(h, row(g_mix[i]), w_r, F32, tm=_tile(n, 512))
        qkv, kt = norm_matmul(h, row(g_mix[i]), w_d, BF16, (c_diff, 2 * c_diff), tm=_tile(n, 512))
        packed = _rwkv_params(i, mu_shift, mu_vres, rw_w0, rw_w2, rw_a0, rw_a2, rw_g2, rw_v0, rw_v2,
                              rw_k_k, rw_k_a, rw_r_k, rw_lnx_w, rw_lnx_b)
        mixed = rwkv_time_mix(z_r.reshape(bsz, seq, zr_cols), *packed, v_first, tb=_tile(seq, 512), nt=4)
        if i == 0:
            y_rwkv, v_first = mixed
        else:
            y_rwkv = mixed
        y_diff = diff_attention(qkv.reshape(bsz, seq, -1), kt, diff_lambda[i], row(diff_subln[i]), _lambda_init(i),
                                tq=_tile(seq, 1024))
        wo = w_out[i].astype(BF16)
        h = out_proj(y_rwkv.reshape(n, c_rwkv), y_diff.reshape(n, -1), wo[:c_rwkv], wo[c_rwkv:], h,
                     tm=_tile(n, 512), tn=d)
        expert_out = []
        if i % 2 == 0:
            j = i // 2
            h = dense_ffn(h, row(g_ffn[i]), dense_w1[j].astype(BF16), dense_w3[j].astype(BF16),
                          dense_w2[j].astype(BF16), tm=_tile(n, 1024), tf=512)
        else:
            expert_out = moe_layer(h, row(g_ffn[i]), router_w[i // 2], *moe_w, i // 2,
                                   tm_router=_tile(n, 512), tm=_tile(n, 1024), tf=512)
        h = ple_layer(h, expert_out, row(g_ple[i]), w_ple_gate[i].astype(BF16), p[i].reshape(n, -1),
                      w_ple_proj[i].astype(BF16), row(g_final), i == depth - 1,
                      tm=_tile(n, 256 if expert_out else 512))
    return h.reshape(bsz, seq, d)
```

```python
import functools
import math

import jax
import jax.numpy as jnp
from jax import lax
from jax.experimental import pallas as pl
from jax.experimental.pallas import tpu as pltpu

F32 = jnp.float32
BF16 = jnp.bfloat16

LANES = 128
VMEM_LIMIT = 56 * 1024 * 1024

RWKV_HEAD = 64
HEADS_PER_TILE = LANES // RWKV_HEAD
DIFF_HEAD_DIM = 64
D_DECAY_LORA, D_AAA_LORA, D_MV_LORA, D_GATE_LORA = 64, 64, 32, 160
LORA_COLS = 3 * LANES
TOP_K = 2
RMS_EPS = 1e-6
GN_EPS = 64e-5
SUBLN_EPS = 1e-5

WKV_CHUNK = 64


def _cparams(sem):
    return pltpu.CompilerParams(dimension_semantics=sem, vmem_limit_bytes=VMEM_LIMIT)


def _dot01_right(x, m01, nsplit=3):
    acc, rem = None, x
    for _ in range(nsplit):
        piece = rem.astype(BF16)
        t = jnp.dot(piece, m01, preferred_element_type=F32)
        acc = t if acc is None else acc + t
        rem = rem - piece.astype(F32)
    return acc


def _split2(x):
    hi = x.astype(BF16)
    return hi, (x - hi.astype(F32)).astype(BF16)


def _bmm_tn(a, b):
    return lax.dot_general(a.astype(BF16), b.astype(BF16), (((1,), (1,)), ((0,), (0,))), preferred_element_type=F32)


def _bmm(a, b):
    return lax.dot_general(a.astype(BF16), b.astype(BF16), (((2,), (1,)), ((0,), (0,))), preferred_element_type=F32)


def _bmm_nt(a, b):
    return lax.dot_general(a.astype(BF16), b.astype(BF16), (((2,), (2,)), ((0,), (0,))), preferred_element_type=F32)


def _bdot01(m01, x, nsplit):
    acc, rem = None, x
    for _ in range(nsplit):
        piece = rem.astype(BF16)
        t = lax.dot_general(m01, piece, (((2,), (1,)), ((0,), (0,))), preferred_element_type=F32)
        acc = t if acc is None else acc + t
        rem = rem - piece.astype(F32)
    return acc


def _dot3(a, b):
    dn = (((1,), (0,)), ((), ()))
    a_hi, a_lo = _split2(a)
    b_hi, b_lo = _split2(b)
    out = lax.dot_general(a_hi, b_hi, dn, preferred_element_type=F32)
    out = out + lax.dot_general(a_hi, b_lo, dn, preferred_element_type=F32)
    return out + lax.dot_general(a_lo, b_hi, dn, preferred_element_type=F32)


def _token_shift(x, prev_row, mu):
    rolled = pltpu.roll(x, 1, 0)
    row = lax.broadcasted_iota(jnp.int32, x.shape, 0)
    prev = jnp.where(row == 0, prev_row, rolled)
    return x + mu * (prev - x)


def _sigmoid(x):
    return 1.0 / (1.0 + jnp.exp(-x))


def _wkv_body(first_layer, tb, nt, *refs):
    if first_layer:
        (zr, zk, zv, zl, par, mul, w2, a2, g2, v2,
         y_out, vf_out, p_s, prev_s, prevl_s, st_s, y_s, wk_s, mn_s) = refs
        vf_in = None
    else:
        (zr, zk, zv, zl, par, mul, w2, a2, g2, v2, vf_in,
         y_out, p_s, prev_s, prevl_s, st_s, y_s, wk_s, mn_s) = refs
    L = WKV_CHUNK
    n_chunks = tb // L

    @pl.when(pl.program_id(2) == 0)
    def _():
        p_s[...] = jnp.zeros_like(p_s)
        prev_s[...] = jnp.zeros_like(prev_s)
        prevl_s[...] = jnp.zeros_like(prevl_s)

    lane = lax.broadcasted_iota(jnp.int32, (LANES, LANES), 1)
    sub = lax.broadcasted_iota(jnp.int32, (LANES, LANES), 0)
    same_head = (lane // RWKV_HEAD) == (sub // RWKV_HEAD)
    block_ones = jnp.where(same_head, 1.0, 0.0).astype(BF16)

    zl_v = zl[...]
    ls = _token_shift(zl_v, prevl_s[...], mul[...])
    prevl_s[...] = zl_v[tb - 1:tb, :]
    l0 = ls[:, 0:LANES]
    l0_tanh = jnp.tanh(l0).astype(BF16)
    l0 = l0.astype(BF16)
    l1_sig = _sigmoid(ls[:, LANES:3 * LANES]).astype(BF16)
    l2 = ls[:, 2 * LANES:3 * LANES].astype(BF16)
    for u in range(nt):
        cols = slice(u * LANES, (u + 1) * LANES)
        rows = slice(u * tb, (u + 1) * tb)
        zr_v, zk_v, zv_v = zr[:, cols], zk[:, cols], zv[:, cols]
        r = _token_shift(zr_v, prev_s[u, 0:1, :], par[0:1, cols])
        k = _token_shift(zk_v, prev_s[u, 1:2, :], par[1:2, cols])
        v = _token_shift(zv_v, prev_s[u, 2:3, :], par[2:3, cols])
        prev_s[u, 0:1, :] = zr_v[tb - 1:tb, :]
        prev_s[u, 1:2, :] = zk_v[tb - 1:tb, :]
        prev_s[u, 2:3, :] = zv_v[tb - 1:tb, :]
        wl = par[3:4, cols] + jnp.dot(l0_tanh, w2[:, cols], preferred_element_type=F32)
        al = par[4:5, cols] + jnp.dot(l0, a2[:, cols], preferred_element_type=F32)
        gate = jnp.dot(l1_sig, g2[:, cols], preferred_element_type=F32)
        if first_layer:
            vf_out[:, cols] = v
        else:
            vg = _sigmoid(par[5:6, cols] + jnp.dot(l2, v2[:, cols], preferred_element_type=F32))
            v = v + (vf_in[:, cols] - v) * vg
        neg_wl = -wl
        softplus = jnp.maximum(neg_wl, 0.0) + jnp.log(1.0 + jnp.exp(-jnp.abs(neg_wl)))
        logw = -jnp.exp(-softplus - 0.5)
        asig = _sigmoid(al)
        kk = k * par[6:7, cols]
        kk_norm = jnp.sqrt(_dot01_right(kk * kk, block_ones))
        kk = kk / jnp.maximum(kk_norm, 1e-12)
        k2 = k * (1.0 + (asig - 1.0) * par[7:8, cols])
        bonus = _dot01_right(r * k2 * par[8:9, cols], block_ones) * v
        st_s[0, rows, :] = r
        st_s[1, rows, :] = logw
        st_s[2, rows, :] = k2
        st_s[3, rows, :] = v
        st_s[4, rows, :] = -kk
        st_s[5, rows, :] = kk * asig
        st_s[6, rows, :] = gate
        st_s[7, rows, :] = bonus

    C = nt * n_chunks
    tb_all = nt * tb
    ti = lax.broadcasted_iota(jnp.int32, (C, L, L), 1)
    si = lax.broadcasted_iota(jnp.int32, (C, L, L), 2)
    tri01 = jnp.where(si <= ti, 1.0, 0.0).astype(BF16)
    ti2 = lax.broadcasted_iota(jnp.int32, (2 * C, 2 * L, L), 1)
    si2 = lax.broadcasted_iota(jnp.int32, (2 * C, 2 * L, L), 2)
    g_mask = si2 <= jnp.where(ti2 < L, ti2 - 1, ti2 - L)
    eye = jnp.where(lax.broadcasted_iota(jnp.int32, (2 * C, L, L), 1)
                    == lax.broadcasted_iota(jnp.int32, (2 * C, L, L), 2), 1.0, 0.0).astype(F32)
    head0 = lax.broadcasted_iota(jnp.int32, (C, L, LANES), 2) < RWKV_HEAD

    chunks = lambda x: x.reshape(C, L, LANES)
    r3, lw3, k3, v3, a3, b3 = (chunks(st_s[i]) for i in range(6))
    cum = _bdot01(tri01, lw3, nsplit=2)
    c_last = cum[:, L - 1:L, :]
    e_inv = jnp.exp(-cum)
    e_end = jnp.exp(c_last - cum)
    a_t = a3 * jnp.exp(cum - lw3)
    b_t = b3 * e_inv
    k_t = k3 * e_inv
    r_t = r3 * jnp.exp(cum)
    two = lambda x: jnp.concatenate([x, x], axis=0)
    per_head = lambda x: jnp.concatenate([jnp.where(head0, x, 0.0), jnp.where(head0, 0.0, x)], axis=0)
    lhs = jnp.concatenate([per_head(a_t), per_head(r_t)], axis=1).astype(BF16)
    g_b = jnp.where(g_mask, _bmm_nt(lhs, two(b_t)), 0.0)
    g_k = jnp.where(g_mask, _bmm_nt(lhs, two(k_t)), 0.0)
    a_ab, a_rb = g_b[:, :L], g_b[:, L:]
    a_ak, a_rk = g_k[:, :L], g_k[:, L:]
    t_inv = eye + a_ab
    pw = _bmm(a_ab, a_ab)
    steps = int(math.log2(L)) - 1
    for j in range(steps):
        if j + 1 < steps:
            both = _bmm(jnp.concatenate([t_inv, pw], axis=1), pw)
            t_inv = t_inv + both[:, :L]
            pw = both[:, L:]
        else:
            t_inv = t_inv + _bmm(t_inv, pw)
    v2x = two(v3)
    ta_h = _bmm(t_inv, two(a_t))
    u0_h = _bmm(t_inv, _bmm(a_ak, v2x))
    rp_h = _bmm(a_rb, ta_h)
    y1_h = _bmm(a_rb, u0_h) + _bmm(a_rk, v2x)
    merge = lambda x: jnp.where(head0, x[:C], x[C:])
    flat = lambda x: x.reshape(tb_all, LANES)
    wk_s[0] = flat(r_t + merge(rp_h))
    wk_s[1] = flat(merge(y1_h))
    b_hat, k_hat = b3 * e_end, k3 * e_end
    ta, u0 = merge(ta_h), merge(u0_h)
    lane3 = lax.broadcasted_iota(jnp.int32, (C, LANES, LANES), 2)
    sub3 = lax.broadcasted_iota(jnp.int32, (C, LANES, LANES), 1)
    same_head3 = (lane3 // RWKV_HEAD) == (sub3 // RWKV_HEAD)
    m3 = _bmm_tn(b_hat, ta) + jnp.where(lane3 == sub3, jnp.exp(c_last), 0.0)
    n3 = _bmm_tn(b_hat, u0) + _bmm_tn(k_hat, v3)
    mn_s[:, :, 0:LANES] = jnp.where(same_head3, m3, 0.0)
    mn_s[:, :, LANES:2 * LANES] = jnp.where(same_head3, n3, 0.0)

    def chain(c, states):
        new = []
        for u in range(nt):
            cu = c + u * n_chunks
            sl = pl.ds(pl.multiple_of(cu * L, L), L)
            y_s[sl, :] = _dot3(wk_s[0, sl, :], states[u]) + wk_s[1, sl, :]
            new.append(_dot3(mn_s[cu, :, 0:LANES], states[u]) + mn_s[cu, :, LANES:2 * LANES])
        return tuple(new)

    states = lax.fori_loop(0, n_chunks, chain, tuple(p_s[u] for u in range(nt)))
    for u in range(nt):
        p_s[u] = states[u]

    inv_n = 1.0 / RWKV_HEAD
    for u in range(nt):
        cols = slice(u * LANES, (u + 1) * LANES)
        rows = slice(u * tb, (u + 1) * tb)
        y = y_s[rows, :]
        mean = _dot01_right(y, block_ones) * inv_n
        d = y - mean
        var = _dot01_right(d * d, block_ones) * inv_n
        yn = d * lax.rsqrt(var + GN_EPS) * par[9:10, cols] + par[10:11, cols]
        y_out[:, cols] = ((yn + st_s[7, rows, :]) * st_s[6, rows, :]).astype(y_out.dtype)


def rwkv_time_mix(z_r, par, mul, w2p, a2p, g2p, v2p, v_first, *, tb, nt):
    bsz, seq, cols = z_r.shape
    c_rwkv = (cols - LORA_COLS) // 3
    n_groups = c_rwkv // (nt * LANES)
    wide = nt * LANES
    first = v_first is None
    grid = (bsz, n_groups, seq // tb)
    col_spec = lambda off: pl.BlockSpec((None, tb, wide), lambda b, h, t: (b, t, off + h))
    par_rows = par.shape[0]
    in_specs = [
        col_spec(0), col_spec(n_groups), col_spec(2 * n_groups),
        pl.BlockSpec((None, tb, LORA_COLS), lambda b, h, t: (b, t, 3 * c_rwkv // LORA_COLS)),
        pl.BlockSpec((par_rows, wide), lambda b, h, t: (0, h)),
        pl.BlockSpec((1, LORA_COLS), lambda b, h, t: (0, 0)),
        pl.BlockSpec((LANES, wide), lambda b, h, t: (0, h)),
        pl.BlockSpec((LANES, wide), lambda b, h, t: (0, h)),
        pl.BlockSpec((2 * LANES, wide), lambda b, h, t: (0, h)),
        pl.BlockSpec((LANES, wide), lambda b, h, t: (0, h)),
    ]
    args = [z_r, z_r, z_r, z_r, par, mul, w2p, a2p, g2p, v2p]
    y_spec = pl.BlockSpec((None, tb, wide), lambda b, h, t: (b, t, h))
    y_shape = jax.ShapeDtypeStruct((bsz, seq, c_rwkv), BF16)
    if first:
        out_shape = (y_shape, jax.ShapeDtypeStruct((bsz, seq, c_rwkv), F32))
        out_specs = (y_spec, y_spec)
    else:
        in_specs.append(y_spec)
        args.append(v_first)
        out_shape = y_shape
        out_specs = y_spec
    scratch = [
        pltpu.VMEM((nt, LANES, LANES), F32),
        pltpu.VMEM((nt, 8, LANES), F32),
        pltpu.VMEM((1, LORA_COLS), F32),
        pltpu.VMEM((8, nt * tb, LANES), F32),
        pltpu.VMEM((nt * tb, LANES), F32),
        pltpu.VMEM((2, nt * tb, LANES), F32),
        pltpu.VMEM((nt * tb // WKV_CHUNK, LANES, 2 * LANES), F32),
    ]
    return pl.pallas_call(
        functools.partial(_wkv_body, first, tb, nt),
        grid=grid, in_specs=in_specs, out_specs=out_specs, out_shape=out_shape,
        scratch_shapes=scratch, name="rwkv7_time_mix",
        compiler_params=_cparams(("parallel", "parallel", "arbitrary")),
    )(*args)


def _rwkv_params(i, mu_shift, mu_vres, rw_w0, rw_w2, rw_a0, rw_a2, rw_g2, rw_v0, rw_v2,
                 rw_k_k, rw_k_a, rw_r_k, rw_lnx_w, rw_lnx_b):
    c = rw_w0.shape[1]
    mu = mu_shift[i]
    zero_c = jnp.zeros((c,), F32)
    rows = [mu[:c], mu[c:2 * c], mu[2 * c:3 * c], rw_w0[i], rw_a0[i],
            rw_v0[i - 1] if i > 0 else zero_c, rw_k_k[i], rw_k_a[i], rw_r_k[i].reshape(-1),
            rw_lnx_w[i], rw_lnx_b[i]]
    par = jnp.stack(rows + [zero_c] * (16 - len(rows)))
    n_lora = D_DECAY_LORA + D_AAA_LORA + D_GATE_LORA
    mu_v = mu_vres[i - 1] if i > 0 else jnp.zeros((D_MV_LORA,), F32)
    mul = jnp.concatenate([mu[3 * c:], mu_v, jnp.zeros((LORA_COLS - n_lora - D_MV_LORA,), F32)])[None, :]
    w2p = jnp.zeros((LANES, c), F32).at[:D_DECAY_LORA].set(rw_w2[i])
    a2p = jnp.zeros((LANES, c), F32).at[D_DECAY_LORA:D_DECAY_LORA + D_AAA_LORA].set(rw_a2[i])
    g2p = jnp.zeros((2 * LANES, c), F32).at[:D_GATE_LORA].set(rw_g2[i])
    v2p = jnp.zeros((LANES, c), F32)
    if i > 0:
        lv0 = n_lora - 2 * LANES
        v2p = v2p.at[lv0:lv0 + D_MV_LORA].set(rw_v2[i - 1])
    return par, mul, w2p.astype(BF16), a2p.astype(BF16), g2p.astype(BF16), v2p.astype(BF16)


def _rmsnorm_rows(x, g):
    ms = jnp.mean(x * x, axis=-1, keepdims=True)
    return x * lax.rsqrt(ms + RMS_EPS) * g


def _norm_matmul_body(t_cols, x_ref, g_ref, w_ref, o_ref, *t_ref):
    u = _rmsnorm_rows(x_ref[...], g_ref[...]).astype(BF16)
    z = jnp.dot(u, w_ref[...], preferred_element_type=F32)
    o_ref[...] = z.astype(o_ref.dtype)
    if t_cols is not None:
        t_ref[0][...] = z[:, t_cols[0]:t_cols[1]].T.astype(t_ref[0].dtype)


def norm_matmul(x, g, w, out_dtype, t_cols=None, *, tm):
    n, d = x.shape
    c = w.shape[1]
    out_shape = [jax.ShapeDtypeStruct((n, c), out_dtype)]
    out_specs = [pl.BlockSpec((tm, c), lambda i: (i, 0))]
    if t_cols is not None:
        ct = t_cols[1] - t_cols[0]
        out_shape.append(jax.ShapeDtypeStruct((ct, n), out_dtype))
        out_specs.append(pl.BlockSpec((ct, tm), lambda i: (0, i)))
    return pl.pallas_call(
        functools.partial(_norm_matmul_body, t_cols),
        grid=(n // tm,),
        in_specs=[pl.BlockSpec((tm, d), lambda i: (i, 0)),
                  pl.BlockSpec((1, d), lambda i: (0, 0)),
                  pl.BlockSpec((d, c), lambda i: (0, 0))],
        out_specs=out_specs, out_shape=out_shape,
        name="norm_matmul",
        compiler_params=_cparams(("parallel",)),
    )(x, g, w)


NEG_BIG = -1e30
ATTN_Q_SCALE = DIFF_HEAD_DIM ** -0.5 * math.log2(math.e)


def _diff_attn_body(tq, lam_init, lp_ref, g_ref, q_ref, kt_ref, v_ref, o_ref):
    qi = pl.program_id(2)
    d = DIFF_HEAD_DIM
    lp = lp_ref[...]
    lam = (jnp.exp(jnp.sum(lp[0:1] * lp[1:2], axis=-1, keepdims=True))
           - jnp.exp(jnp.sum(lp[2:3] * lp[3:4], axis=-1, keepdims=True)) + lam_init)
    q = q_ref[...]
    lane = lax.broadcasted_iota(jnp.int32, q.shape, 1)
    q_maps = (jnp.where(lane < d, q, jnp.zeros_like(q)), jnp.where(lane >= d, q, jnp.zeros_like(q)))

    def step(j, carry, masked):
        start = pl.multiple_of(j * tq, tq)
        kt = kt_ref[:, pl.ds(start, tq)]
        vb = v_ref[pl.ds(start, tq), :]
        new = []
        for c in range(2):
            m, l, acc = carry[c]
            s = jnp.dot(q_maps[c], kt, preferred_element_type=F32)
            if masked:
                row = lax.broadcasted_iota(jnp.int32, s.shape, 0)
                col = lax.broadcasted_iota(jnp.int32, s.shape, 1)
                s = jnp.where(col <= row, s, NEG_BIG)
            m_new = jnp.maximum(m, jnp.max(s, axis=-1, keepdims=True))
            alpha = jnp.exp2(m - m_new)
            pr = jnp.exp2(s - m_new)
            l_new = alpha * l + jnp.sum(pr, axis=-1, keepdims=True)
            acc_new = alpha * acc + jnp.dot(pr.astype(BF16), vb, preferred_element_type=F32)
            new.append((m_new, l_new, acc_new))
        return tuple(new)

    init_one = (jnp.full((tq, 1), NEG_BIG, F32), jnp.zeros((tq, 1), F32), jnp.zeros((tq, LANES), F32))
    carry = lax.fori_loop(0, qi, lambda j, c: step(j, c, False), (init_one, init_one))
    (_, l1, acc1), (_, l2, acc2) = step(qi, carry, True)
    o = acc1 / l1 - lam * (acc2 / l2)
    o = o * lax.rsqrt(jnp.mean(o * o, axis=-1, keepdims=True) + SUBLN_EPS)
    o_ref[...] = (o * g_ref[...] * (1.0 - lam_init)).astype(o_ref.dtype)


def diff_attention(qkv, kt, lam_params, subln_g, lam_init, *, tq):
    bsz, seq, c3 = qkv.shape
    n_heads = c3 // (3 * LANES)
    return pl.pallas_call(
        functools.partial(_diff_attn_body, tq, lam_init),
        grid=(bsz, n_heads, seq // tq),
        in_specs=[pl.BlockSpec(lam_params.shape, lambda b, h, i: (0, 0)),
                  pl.BlockSpec((1, LANES), lambda b, h, i: (0, 0)),
                  pl.BlockSpec((None, tq, LANES), lambda b, h, i: (b, i, h)),
                  pl.BlockSpec((LANES, seq), lambda b, h, i: (h, b)),
                  pl.BlockSpec((None, seq, LANES), lambda b, h, i: (b, 0, 2 * n_heads + h))],
        out_specs=pl.BlockSpec((None, tq, LANES), lambda b, h, i: (b, i, h)),
        out_shape=jax.ShapeDtypeStruct((bsz, seq, n_heads * LANES), BF16),
        name="diff_attention",
        compiler_params=_cparams(("parallel", "parallel", "arbitrary")),
    )(lam_params, subln_g, qkv, kt, qkv)


def _out_proj_body(ya_ref, yb_ref, wa_ref, wb_ref, h_ref, o_ref):
    acc = jnp.dot(ya_ref[...], wa_ref[...], preferred_element_type=F32)
    acc = acc + jnp.dot(yb_ref[...], wb_ref[...], preferred_element_type=F32)
    o_ref[...] = h_ref[...] + acc


def out_proj(ya, yb, wa, wb, h, *, tm, tn):
    n, d = h.shape
    ca, cb = ya.shape[1], yb.shape[1]
    return pl.pallas_call(
        _out_proj_body,
        grid=(n // tm, d // tn),
        in_specs=[pl.BlockSpec((tm, ca), lambda i, j: (i, 0)),
                  pl.BlockSpec((tm, cb), lambda i, j: (i, 0)),
                  pl.BlockSpec((ca, tn), lambda i, j: (0, j)),
                  pl.BlockSpec((cb, tn), lambda i, j: (0, j)),
                  pl.BlockSpec((tm, tn), lambda i, j: (i, j))],
        out_specs=pl.BlockSpec((tm, tn), lambda i, j: (i, j)),
        out_shape=jax.ShapeDtypeStruct((n, d), F32),
        name="out_proj",
        compiler_params=_cparams(("parallel", "parallel")),
    )(ya, yb, wa, wb, h)


def _swiglu_tile(u, w1_ref, w3_ref, w2_ref):
    a = jnp.dot(u, w1_ref[...], preferred_element_type=F32)
    b = jnp.dot(u, w3_ref[...], preferred_element_type=F32)
    mid = (a * _sigmoid(a) * b).astype(BF16)
    return jnp.dot(mid, w2_ref[...], preferred_element_type=F32)


def _dense_ffn_body(x_ref, g_ref, w1_ref, w3_ref, w2_ref, o_ref, u_s):
    @pl.when(pl.program_id(1) == 0)
    def _():
        x = x_ref[...]
        u_s[...] = _rmsnorm_rows(x, g_ref[...]).astype(BF16)
        o_ref[...] = x

    o_ref[...] += _swiglu_tile(u_s[...], w1_ref, w3_ref, w2_ref)


def dense_ffn(x, g, w1, w3, w2, *, tm, tf):
    n, d = x.shape
    ff = w1.shape[1]
    assert n % tm == 0 and ff % tf == 0
    return pl.pallas_call(
        _dense_ffn_body,
        grid=(n // tm, ff // tf),
        in_specs=[pl.BlockSpec((tm, d), lambda i, f: (i, 0)),
                  pl.BlockSpec((1, d), lambda i, f: (0, 0)),
                  pl.BlockSpec((d, tf), lambda i, f: (0, f)),
                  pl.BlockSpec((d, tf), lambda i, f: (0, f)),
                  pl.BlockSpec((tf, d), lambda i, f: (f, 0))],
        out_specs=pl.BlockSpec((tm, d), lambda i, f: (i, 0)),
        out_shape=jax.ShapeDtypeStruct((n, d), F32),
        scratch_shapes=[pltpu.VMEM((tm, d), BF16)],
        name="dense_ffn",
        compiler_params=_cparams(("parallel", "arbitrary")),
    )(x, g, w1, w3, w2)


ROUTE_ROWS = 8


def _router_body(n_experts, x_ref, g_ref, wr_ref, u_ref, route_ref):
    u = _rmsnorm_rows(x_ref[...], g_ref[...])
    u_ref[...] = u.astype(BF16)
    u_hi, u_lo = _split2(u)
    wr = wr_ref[...]
    w_hi, w_lo = _split2(wr)
    logits = (jnp.dot(u_hi, w_hi, preferred_element_type=F32) + jnp.dot(u_hi, w_lo, preferred_element_type=F32)
              + jnp.dot(u_lo, w_hi, preferred_element_type=F32))
    lane = lax.broadcasted_iota(jnp.int32, logits.shape, 1)
    logits = jnp.where(lane < n_experts, logits, NEG_BIG)
    top1 = jnp.max(logits, axis=-1, keepdims=True)
    idx1 = jnp.min(jnp.where(logits == top1, lane, LANES), axis=-1, keepdims=True)
    rest = jnp.where(lane == idx1, NEG_BIG, logits)
    top2 = jnp.max(rest, axis=-1, keepdims=True)
    idx2 = jnp.min(jnp.where(rest == top2, lane, LANES), axis=-1, keepdims=True)
    e2 = jnp.exp(top2 - top1)
    gate1 = 1.0 / (1.0 + e2)
    slab = jnp.where(lane == 0, idx1.astype(F32),
                     jnp.where(lane == 1, idx2.astype(F32),
                               jnp.where(lane == 2, gate1, jnp.where(lane == 3, e2 * gate1, 0.0))))
    route_ref[...] = slab.T[:ROUTE_ROWS, :]


def moe_router(x, g, router_w, *, tm):
    n, d = x.shape
    n_experts = router_w.shape[1]
    wr = jnp.zeros((d, LANES), F32).at[:, :n_experts].set(router_w)
    return pl.pallas_call(
        functools.partial(_router_body, n_experts),
        grid=(n // tm,),
        in_specs=[pl.BlockSpec((tm, d), lambda i: (i, 0)),
                  pl.BlockSpec((1, d), lambda i: (0, 0)),
                  pl.BlockSpec((d, LANES), lambda i: (0, 0))],
        out_specs=(pl.BlockSpec((tm, d), lambda i: (i, 0)),
                   pl.BlockSpec((ROUTE_ROWS, tm), lambda i: (0, i))),
        out_shape=(jax.ShapeDtypeStruct((n, d), BF16),
                   jax.ShapeDtypeStruct((ROUTE_ROWS, n), F32)),
        name="moe_router",
        compiler_params=_cparams(("parallel",)),
    )(x, g, wr)


def _moe_ffn_body(be_ref, nu_ref, x_ref, gate_ref, w1_ref, w3_ref, w2_ref, o_ref, acc_s):
    i, f = pl.program_id(0), pl.program_id(1)
    used = i < nu_ref[0]

    @pl.when(jnp.logical_and(used, f == 0))
    def _():
        acc_s[...] = jnp.zeros_like(acc_s)

    @pl.when(used)
    def _():
        acc_s[...] += _swiglu_tile(x_ref[...], w1_ref, w3_ref, w2_ref)

    @pl.when(f == pl.num_programs(1) - 1)
    def _():
        o_ref[...] = jnp.where(used, acc_s[...] * gate_ref[...], 0.0)


def moe_ffn(x_rows, row_gate, blk_expert, n_used, w1, w3, w2, layer, *, tm, tf):
    rows, d = x_rows.shape
    ff = w1.shape[3]
    nf = ff // tf

    def f_idx(i, f, nu):
        return jnp.where(i < nu[0], f, nf - 1)

    grid_spec = pltpu.PrefetchScalarGridSpec(
        num_scalar_prefetch=2,
        grid=(rows // tm, nf),
        in_specs=[pl.BlockSpec((tm, d), lambda i, f, be, nu: (i, 0)),
                  pl.BlockSpec((tm, 1), lambda i, f, be, nu: (i, 0)),
                  pl.BlockSpec((None, None, d, tf), lambda i, f, be, nu: (layer, be[i], 0, f_idx(i, f, nu))),
                  pl.BlockSpec((None, None, d, tf), lambda i, f, be, nu: (layer, be[i], 0, f_idx(i, f, nu))),
                  pl.BlockSpec((None, None, tf, d), lambda i, f, be, nu: (layer, be[i], f_idx(i, f, nu), 0))],
        out_specs=pl.BlockSpec((tm, d), lambda i, f, be, nu: (i, 0)),
        scratch_shapes=[pltpu.VMEM((tm, d), F32)],
    )
    return pl.pallas_call(
        _moe_ffn_body, grid_spec=grid_spec,
        out_shape=jax.ShapeDtypeStruct((rows, d), F32),
        name="moe_ffn",
        compiler_params=_cparams(("arbitrary", "arbitrary")),
    )(blk_expert, n_used, x_rows, row_gate, w1, w3, w2)


def moe_layer(h, g, router_w, w1, w3, w2, layer, *, tm_router, tm, tf):
    n, d = h.shape
    n_experts = router_w.shape[1]
    u, route = moe_router(h, g, router_w, tm=tm_router)
    e_flat = route[:TOP_K].T.reshape(-1).astype(jnp.int32)
    g_flat = route[TOP_K:2 * TOP_K].T.reshape(-1)
    m = n * TOP_K
    onehot = (e_flat[:, None] == jnp.arange(n_experts, dtype=jnp.int32)[None, :]).astype(jnp.int32)
    counts = jnp.sum(onehot, axis=0)
    rank = jnp.sum((jnp.cumsum(onehot, axis=0) - onehot) * onehot, axis=1)
    padded = (counts + tm - 1) // tm * tm
    pad_ends = jnp.cumsum(padded)
    dest = (pad_ends - padded)[e_flat] + rank
    n_blocks = m // tm + n_experts
    rows = n_blocks * tm
    row_slot = jnp.full((rows,), -1, jnp.int32).at[dest].set(jnp.arange(m, dtype=jnp.int32))
    row_tok = jnp.maximum(row_slot, 0) // TOP_K
    row_gate = jnp.where(row_slot >= 0, g_flat[jnp.maximum(row_slot, 0)], 0.0)
    blk_start = jnp.arange(n_blocks, dtype=jnp.int32) * tm
    blk_expert = jnp.minimum(jnp.sum(blk_start[:, None] >= pad_ends[None, :], axis=1), n_experts - 1).astype(jnp.int32)
    n_used = (pad_ends[-1:] // tm).astype(jnp.int32)
    y_rows = moe_ffn(u[row_tok], row_gate[:, None], blk_expert, n_used, w1, w3, w2, layer, tm=tm, tf=tf)
    dest = dest.reshape(n, TOP_K)
    return [y_rows[dest[:, k]] for k in range(TOP_K)]


def _ple_body(final_norm, n_add, x_ref, *refs):
    g_ref, wg_ref, p_ref, wp_ref, gf_ref, o_ref = refs[n_add:]
    x = x_ref[...]
    for add_ref in refs[:n_add]:
        x = x + add_ref[...]
    u = _rmsnorm_rows(x, g_ref[...]).astype(BF16)
    gate = _sigmoid(jnp.dot(u, wg_ref[...], preferred_element_type=F32))
    proj = jnp.dot(p_ref[...].astype(BF16), wp_ref[...], preferred_element_type=F32)
    out = x + gate * proj
    if final_norm:
        out = _rmsnorm_rows(out, gf_ref[...])
    o_ref[...] = out


def ple_layer(x, addends, g, w_gate, p, w_proj, g_final, final_norm, *, tm):
    n, d = x.shape
    dp = p.shape[1]
    add_args = list(addends)
    add_specs = [pl.BlockSpec((tm, d), lambda i: (i, 0)) for _ in add_args]
    return pl.pallas_call(
        functools.partial(_ple_body, final_norm, len(add_args)),
        grid=(n // tm,),
        in_specs=[pl.BlockSpec((tm, d), lambda i: (i, 0))] + add_specs + [
                  pl.BlockSpec((1, d), lambda i: (0, 0)),
                  pl.BlockSpec((d, d), lambda i: (0, 0)),
                  pl.BlockSpec((tm, dp), lambda i: (i, 0)),
                  pl.BlockSpec((dp, d), lambda i: (0, 0)),
                  pl.BlockSpec((1, d), lambda i: (0, 0))],
        out_specs=pl.BlockSpec((tm, d), lambda i: (i, 0)),
        out_shape=jax.ShapeDtypeStruct((n, d), F32),
        name="ple_layer",
        compiler_params=_cparams(("parallel",)),
    )(x, *add_args, g, w_gate, p, w_proj, g_final)


def _lambda_init(layer_idx):
    return 0.8 - 0.6 * math.exp(-0.3 * layer_idx)


def _tile(n, pref):
    t = min(n, pref)
    while n % t:
        t //= 2
    return t


def kernel(x, p, g_mix, w_in, w_vres_down, mu_shift, mu_vres, rw_w0, rw_w2, rw_a0, rw_a2, rw_g2, rw_v0, rw_v2, rw_k_k, rw_k_a, rw_r_k, rw_lnx_w, rw_lnx_b, diff_lambda, diff_subln, w_out, g_ffn, dense_w1, dense_w3, dense_w2, router_w, moe_w1, moe_w3, moe_w2, g_ple, w_ple_gate, w_ple_proj, g_final):
    bsz, seq, d = x.shape
    depth = w_in.shape[0]
    n = bsz * seq
    c_rwkv = rw_w0.shape[1]
    shift_cols = mu_shift.shape[1]
    zr_cols = 3 * c_rwkv + LORA_COLS
    row = lambda v: v.reshape(1, -1)
    tm = _tile(n, 1024)
    h = x.reshape(n, d)
    moe_w = (moe_w1.astype(BF16), moe_w3.astype(BF16), moe_w2.astype(BF16))
    v_first = None
    for i in range(depth):
        w_vres = w_vres_down[i - 1] if i > 0 else jnp.zeros((d, D_MV_LORA), F32)
        w_r = jnp.concatenate([w_in[i][:, :shift_cols], w_vres,
                               jnp.zeros((d, zr_cols - shift_cols - D_MV_LORA), F32)], axis=1).astype(BF16)
        w_d = w_in[i][:, shift_cols:]
        c_diff = w_d.shape[1] // 3
        w_d = jnp.concatenate([w_d[:, :c_diff] * ATTN_Q_SCALE, w_d[:, c_diff:]], axis=1).astype(BF16)
        z_r, = norm_matmul(h, row(g_mix[i]), w_r, F32, tm=_tile(n, 512))
        qkv, kt = norm_matmul(h, row(g_mix[i]), w_d, BF16, (c_diff, 2 * c_diff), tm=_tile(n, 512))
        packed = _rwkv_params(i, mu_shift, mu_vres, rw_w0, rw_w2, rw_a0, rw_a2, rw_g2, rw_v0, rw_v2,
                              rw_k_k, rw_k_a, rw_r_k, rw_lnx_w, rw_lnx_b)
        mixed = rwkv_time_mix(z_r.reshape(bsz, seq, zr_cols), *packed, v_first, tb=_tile(seq, 512), nt=4)
        if i == 0:
            y_rwkv, v_first = mixed
        else:
            y_rwkv = mixed
        y_diff = diff_attention(qkv.reshape(bsz, seq, -1), kt, diff_lambda[i], row(diff_subln[i]), _lambda_init(i),
                                tq=_tile(seq, 1024))
        wo = w_out[i].astype(BF16)
        h = out_proj(y_rwkv.reshape(n, c_rwkv), y_diff.reshape(n, -1), wo[:c_rwkv], wo[c_rwkv:], h,
                     tm=_tile(n, 512), tn=d)
        expert_out = []
        if i % 2 == 0:
            j = i // 2
            h = dense_ffn(h, row(g_ffn[i]), dense_w1[j].astype(BF16), dense_w3[j].astype(BF16),
                          dense_w2[j].astype(BF16), tm=_tile(n, 1024), tf=512)
        else:
            expert_out = moe_layer(h, row(g_ffn[i]), router_w[i // 2], *moe_w, i // 2,
                                   tm_router=_tile(n, 512), tm=_tile(n, 1024), tf=512)
        h = ple_layer(h, expert_out, row(g_ple[i]), w_ple_gate[i].astype(BF16), p[i].reshape(n, -1),
                      w_ple_proj[i].astype(BF16), row(g_final), i == depth - 1,
                      tm=_tile(n, 256 if expert_out else 512))
    return h.reshape(bsz, seq, d)
```

```python
import functools
import math

import jax
import jax.numpy as jnp
from jax import lax
from jax.experimental import pallas as pl
from jax.experimental.pallas import tpu as pltpu

F32 = jnp.float32
BF16 = jnp.bfloat16

LANES = 128
VMEM_LIMIT = 56 * 1024 * 1024

RWKV_HEAD = 64
HEADS_PER_TILE = LANES // RWKV_HEAD
DIFF_HEAD_DIM = 64
D_DECAY_LORA, D_AAA_LORA, D_MV_LORA, D_GATE_LORA = 64, 64, 32, 160
LORA_COLS = 3 * LANES
TOP_K = 2
RMS_EPS = 1e-6
GN_EPS = 64e-5
SUBLN_EPS = 1e-5

WKV_CHUNK = 64


MOE_VMEM_LIMIT = 62 * 1024 * 1024


def _cparams(sem, vmem_limit=VMEM_LIMIT):
    return pltpu.CompilerParams(dimension_semantics=sem, vmem_limit_bytes=vmem_limit)


def _dot01_right(x, m01, nsplit=3):
    acc, rem = None, x
    for _ in range(nsplit):
        piece = rem.astype(BF16)
        t = jnp.dot(piece, m01, preferred_element_type=F32)
        acc = t if acc is None else acc + t
        rem = rem - piece.astype(F32)
    return acc


def _split2(x):
    hi = x.astype(BF16)
    return hi, (x - hi.astype(F32)).astype(BF16)


def _bmm_tn(a, b):
    return lax.dot_general(a.astype(BF16), b.astype(BF16), (((1,), (1,)), ((0,), (0,))), preferred_element_type=F32)


def _bmm(a, b):
    return lax.dot_general(a.astype(BF16), b.astype(BF16), (((2,), (1,)), ((0,), (0,))), preferred_element_type=F32)


def _bmm_nt(a, b):
    return lax.dot_general(a.astype(BF16), b.astype(BF16), (((2,), (2,)), ((0,), (0,))), preferred_element_type=F32)


def _bdot01(m01, x, nsplit):
    acc, rem = None, x
    for _ in range(nsplit):
        piece = rem.astype(BF16)
        t = lax.dot_general(m01, piece, (((2,), (1,)), ((0,), (0,))), preferred_element_type=F32)
        acc = t if acc is None else acc + t
        rem = rem - piece.astype(F32)
    return acc


def _dot3(a, b):
    dn = (((1,), (0,)), ((), ()))
    a_hi, a_lo = _split2(a)
    b_hi, b_lo = _split2(b)
    out = lax.dot_general(a_hi, b_hi, dn, preferred_element_type=F32)
    out = out + lax.dot_general(a_hi, b_lo, dn, preferred_element_type=F32)
    return out + lax.dot_general(a_lo, b_hi, dn, preferred_element_type=F32)


def _token_shift(x, prev_row, mu):
    rolled = pltpu.roll(x, 1, 0)
    row = lax.broadcasted_iota(jnp.int32, x.shape, 0)
    prev = jnp.where(row == 0, prev_row, rolled)
    return x + mu * (prev - x)


def _sigmoid(x):
    return 1.0 / (1.0 + jnp.exp(-x))


def _wkv_body(first_layer, tb, nt, *refs):
    if first_layer:
        (zr, zk, zv, zl, par, mul, w2, a2, g2, v2,
         y_out, vf_out, p_s, prev_s, prevl_s, st_s, y_s, wk_s, mn_s) = refs
        vf_in = None
    else:
        (zr, zk, zv, zl, par, mul, w2, a2, g2, v2, vf_in,
         y_out, p_s, prev_s, prevl_s, st_s, y_s, wk_s, mn_s) = refs
    L = WKV_CHUNK
    n_chunks = tb // L

    @pl.when(pl.program_id(2) == 0)
    def _():
        p_s[...] = jnp.zeros_like(p_s)
        prev_s[...] = jnp.zeros_like(prev_s)
        prevl_s[...] = jnp.zeros_like(prevl_s)

    lane = lax.broadcasted_iota(jnp.int32, (LANES, LANES), 1)
    sub = lax.broadcasted_iota(jnp.int32, (LANES, LANES), 0)
    same_head = (lane // RWKV_HEAD) == (sub // RWKV_HEAD)
    block_ones = jnp.where(same_head, 1.0, 0.0).astype(BF16)

    zl_v = zl[...]
    ls = _token_shift(zl_v, prevl_s[...], mul[...])
    prevl_s[...] = zl_v[tb - 1:tb, :]
    l0 = ls[:, 0:LANES]
    l0_tanh = jnp.tanh(l0).astype(BF16)
    l0 = l0.astype(BF16)
    l1_sig = _sigmoid(ls[:, LANES:3 * LANES]).astype(BF16)
    l2 = ls[:, 2 * LANES:3 * LANES].astype(BF16)
    for u in range(nt):
        cols = slice(u * LANES, (u + 1) * LANES)
        rows = slice(u * tb, (u + 1) * tb)
        zr_v, zk_v, zv_v = zr[:, cols], zk[:, cols], zv[:, cols]
        r = _token_shift(zr_v, prev_s[u, 0:1, :], par[0:1, cols])
        k = _token_shift(zk_v, prev_s[u, 1:2, :], par[1:2, cols])
        v = _token_shift(zv_v, prev_s[u, 2:3, :], par[2:3, cols])
        prev_s[u, 0:1, :] = zr_v[tb - 1:tb, :]
        prev_s[u, 1:2, :] = zk_v[tb - 1:tb, :]
        prev_s[u, 2:3, :] = zv_v[tb - 1:tb, :]
        wl = par[3:4, cols] + jnp.dot(l0_tanh, w2[:, cols], preferred_element_type=F32)
        al = par[4:5, cols] + jnp.dot(l0, a2[:, cols], preferred_element_type=F32)
        gate = jnp.dot(l1_sig, g2[:, cols], preferred_element_type=F32)
        if first_layer:
            vf_out[:, cols] = v
        else:
            vg = _sigmoid(par[5:6, cols] + jnp.dot(l2, v2[:, cols], preferred_element_type=F32))
            v = v + (vf_in[:, cols] - v) * vg
        neg_wl = -wl
        softplus = jnp.maximum(neg_wl, 0.0) + jnp.log(1.0 + jnp.exp(-jnp.abs(neg_wl)))
        logw = -jnp.exp(-softplus - 0.5)
        asig = _sigmoid(al)
        kk = k * par[6:7, cols]
        kk_norm = jnp.sqrt(_dot01_right(kk * kk, block_ones))
        kk = kk / jnp.maximum(kk_norm, 1e-12)
        k2 = k * (1.0 + (asig - 1.0) * par[7:8, cols])
        bonus = _dot01_right(r * k2 * par[8:9, cols], block_ones) * v
        st_s[0, rows, :] = r
        st_s[1, rows, :] = logw
        st_s[2, rows, :] = k2
        st_s[3, rows, :] = v
        st_s[4, rows, :] = -kk
        st_s[5, rows, :] = kk * asig
        st_s[6, rows, :] = gate
        st_s[7, rows, :] = bonus

    C = nt * n_chunks
    tb_all = nt * tb
    ti = lax.broadcasted_iota(jnp.int32, (C, L, L), 1)
    si = lax.broadcasted_iota(jnp.int32, (C, L, L), 2)
    tri01 = jnp.where(si <= ti, 1.0, 0.0).astype(BF16)
    ti2 = lax.broadcasted_iota(jnp.int32, (2 * C, 2 * L, L), 1)
    si2 = lax.broadcasted_iota(jnp.int32, (2 * C, 2 * L, L), 2)
    g_mask = si2 <= jnp.where(ti2 < L, ti2 - 1, ti2 - L)
    eye = jnp.where(lax.broadcasted_iota(jnp.int32, (2 * C, L, L), 1)
                    == lax.broadcasted_iota(jnp.int32, (2 * C, L, L), 2), 1.0, 0.0).astype(F32)
    head0 = lax.broadcasted_iota(jnp.int32, (C, L, LANES), 2) < RWKV_HEAD

    chunks = lambda x: x.reshape(C, L, LANES)
    r3, lw3, k3, v3, a3, b3 = (chunks(st_s[i]) for i in range(6))
    cum = _bdot01(tri01, lw3, nsplit=2)
    c_last = cum[:, L - 1:L, :]
    e_inv = jnp.exp(-cum)
    e_end = jnp.exp(c_last - cum)
    a_t = a3 * jnp.exp(cum - lw3)
    b_t = b3 * e_inv
    k_t = k3 * e_inv
    r_t = r3 * jnp.exp(cum)
    two = lambda x: jnp.concatenate([x, x], axis=0)
    per_head = lambda x: jnp.concatenate([jnp.where(head0, x, 0.0), jnp.where(head0, 0.0, x)], axis=0)
    lhs = jnp.concatenate([per_head(a_t), per_head(r_t)], axis=1).astype(BF16)
    g_b = jnp.where(g_mask, _bmm_nt(lhs, two(b_t)), 0.0)
    g_k = jnp.where(g_mask, _bmm_nt(lhs, two(k_t)), 0.0)
    a_ab, a_rb = g_b[:, :L], g_b[:, L:]
    a_ak, a_rk = g_k[:, :L], g_k[:, L:]
    t_inv = eye + a_ab
    pw = _bmm(a_ab, a_ab)
    steps = int(math.log2(L)) - 1
    for j in range(steps):
        if j + 1 < steps:
            both = _bmm(jnp.concatenate([t_inv, pw], axis=1), pw)
            t_inv = t_inv + both[:, :L]
            pw = both[:, L:]
        else:
            t_inv = t_inv + _bmm(t_inv, pw)
    v2x = two(v3)
    ta_h = _bmm(t_inv, two(a_t))
    u0_h = _bmm(t_inv, _bmm(a_ak, v2x))
    rp_h = _bmm(a_rb, ta_h)
    y1_h = _bmm(a_rb, u0_h) + _bmm(a_rk, v2x)
    merge = lambda x: jnp.where(head0, x[:C], x[C:])
    flat = lambda x: x.reshape(tb_all, LANES)
    wk_s[0] = flat(r_t + merge(rp_h))
    wk_s[1] = flat(merge(y1_h))
    b_hat, k_hat = b3 * e_end, k3 * e_end
    ta, u0 = merge(ta_h), merge(u0_h)
    lane3 = lax.broadcasted_iota(jnp.int32, (C, LANES, LANES), 2)
    sub3 = lax.broadcasted_iota(jnp.int32, (C, LANES, LANES), 1)
    same_head3 = (lane3 // RWKV_HEAD) == (sub3 // RWKV_HEAD)
    m3 = _bmm_tn(b_hat, ta) + jnp.where(lane3 == sub3, jnp.exp(c_last), 0.0)
    n3 = _bmm_tn(b_hat, u0) + _bmm_tn(k_hat, v3)
    mn_s[:, :, 0:LANES] = jnp.where(same_head3, m3, 0.0)
    mn_s[:, :, LANES:2 * LANES] = jnp.where(same_head3, n3, 0.0)

    def chain(c, states):
        new = []
        for u in range(nt):
            cu = c + u * n_chunks
            sl = pl.ds(pl.multiple_of(cu * L, L), L)
            y_s[sl, :] = _dot3(wk_s[0, sl, :], states[u]) + wk_s[1, sl, :]
            new.append(_dot3(mn_s[cu, :, 0:LANES], states[u]) + mn_s[cu, :, LANES:2 * LANES])
        return tuple(new)

    states = lax.fori_loop(0, n_chunks, chain, tuple(p_s[u] for u in range(nt)))
    for u in range(nt):
        p_s[u] = states[u]

    inv_n = 1.0 / RWKV_HEAD
    for u in range(nt):
        cols = slice(u * LANES, (u + 1) * LANES)
        rows = slice(u * tb, (u + 1) * tb)
        y = y_s[rows, :]
        mean = _dot01_right(y, block_ones) * inv_n
        d = y - mean
        var = _dot01_right(d * d, block_ones) * inv_n
        yn = d * lax.rsqrt(var + GN_EPS) * par[9:10, cols] + par[10:11, cols]
        y_out[:, cols] = ((yn + st_s[7, rows, :]) * st_s[6, rows, :]).astype(y_out.dtype)


def rwkv_time_mix(z_r, par, mul, w2p, a2p, g2p, v2p, v_first, *, tb, nt):
    bsz, seq, cols = z_r.shape
    c_rwkv = (cols - LORA_COLS) // 3
    n_groups = c_rwkv // (nt * LANES)
    wide = nt * LANES
    first = v_first is None
    grid = (bsz, n_groups, seq // tb)
    col_spec = lambda off: pl.BlockSpec((None, tb, wide), lambda b, h, t: (b, t, off + h))
    par_rows = par.shape[0]
    in_specs = [
        col_spec(0), col_spec(n_groups), col_spec(2 * n_groups),
        pl.BlockSpec((None, tb, LORA_COLS), lambda b, h, t: (b, t, 3 * c_rwkv // LORA_COLS)),
        pl.BlockSpec((par_rows, wide), lambda b, h, t: (0, h)),
        pl.BlockSpec((1, LORA_COLS), lambda b, h, t: (0, 0)),
        pl.BlockSpec((LANES, wide), lambda b, h, t: (0, h)),
        pl.BlockSpec((LANES, wide), lambda b, h, t: (0, h)),
        pl.BlockSpec((2 * LANES, wide), lambda b, h, t: (0, h)),
        pl.BlockSpec((LANES, wide), lambda b, h, t: (0, h)),
    ]
    args = [z_r, z_r, z_r, z_r, par, mul, w2p, a2p, g2p, v2p]
    y_spec = pl.BlockSpec((None, tb, wide), lambda b, h, t: (b, t, h))
    y_shape = jax.ShapeDtypeStruct((bsz, seq, c_rwkv), BF16)
    if first:
        out_shape = (y_shape, jax.ShapeDtypeStruct((bsz, seq, c_rwkv), F32))
        out_specs = (y_spec, y_spec)
    else:
        in_specs.append(y_spec)
        args.append(v_first)
        out_shape = y_shape
        out_specs = y_spec
    scratch = [
        pltpu.VMEM((nt, LANES, LANES), F32),
        pltpu.VMEM((nt, 8, LANES), F32),
        pltpu.VMEM((1, LORA_COLS), F32),
        pltpu.VMEM((8, nt * tb, LANES), F32),
        pltpu.VMEM((nt * tb, LANES), F32),
        pltpu.VMEM((2, nt * tb, LANES), F32),
        pltpu.VMEM((nt * tb // WKV_CHUNK, LANES, 2 * LANES), F32),
    ]
    return pl.pallas_call(
        functools.partial(_wkv_body, first, tb, nt),
        grid=grid, in_specs=in_specs, out_specs=out_specs, out_shape=out_shape,
        scratch_shapes=scratch, name="rwkv7_time_mix",
        compiler_params=_cparams(("parallel", "parallel", "arbitrary")),
    )(*args)


def _rwkv_params(i, mu_shift, mu_vres, rw_w0, rw_w2, rw_a0, rw_a2, rw_g2, rw_v0, rw_v2,
                 rw_k_k, rw_k_a, rw_r_k, rw_lnx_w, rw_lnx_b):
    c = rw_w0.shape[1]
    mu = mu_shift[i]
    zero_c = jnp.zeros((c,), F32)
    rows = [mu[:c], mu[c:2 * c], mu[2 * c:3 * c], rw_w0[i], rw_a0[i],
            rw_v0[i - 1] if i > 0 else zero_c, rw_k_k[i], rw_k_a[i], rw_r_k[i].reshape(-1),
            rw_lnx_w[i], rw_lnx_b[i]]
    par = jnp.stack(rows + [zero_c] * (16 - len(rows)))
    n_lora = D_DECAY_LORA + D_AAA_LORA + D_GATE_LORA
    mu_v = mu_vres[i - 1] if i > 0 else jnp.zeros((D_MV_LORA,), F32)
    mul = jnp.concatenate([mu[3 * c:], mu_v, jnp.zeros((LORA_COLS - n_lora - D_MV_LORA,), F32)])[None, :]
    w2p = jnp.zeros((LANES, c), F32).at[:D_DECAY_LORA].set(rw_w2[i])
    a2p = jnp.zeros((LANES, c), F32).at[D_DECAY_LORA:D_DECAY_LORA + D_AAA_LORA].set(rw_a2[i])
    g2p = jnp.zeros((2 * LANES, c), F32).at[:D_GATE_LORA].set(rw_g2[i])
    v2p = jnp.zeros((LANES, c), F32)
    if i > 0:
        lv0 = n_lora - 2 * LANES
        v2p = v2p.at[lv0:lv0 + D_MV_LORA].set(rw_v2[i - 1])
    return par, mul, w2p.astype(BF16), a2p.astype(BF16), g2p.astype(BF16), v2p.astype(BF16)


def _rmsnorm_rows(x, g):
    ms = jnp.mean(x * x, axis=-1, keepdims=True)
    return x * lax.rsqrt(ms + RMS_EPS) * g


def _norm_matmul_body(t_cols, x_ref, g_ref, w_ref, o_ref, *t_ref):
    u = _rmsnorm_rows(x_ref[...], g_ref[...]).astype(BF16)
    z = jnp.dot(u, w_ref[...], preferred_element_type=F32)
    o_ref[...] = z.astype(o_ref.dtype)
    if t_cols is not None:
        t_ref[0][...] = z[:, t_cols[0]:t_cols[1]].T.astype(t_ref[0].dtype)


def norm_matmul(x, g, w, out_dtype, t_cols=None, *, tm):
    n, d = x.shape
    c = w.shape[1]
    out_shape = [jax.ShapeDtypeStruct((n, c), out_dtype)]
    out_specs = [pl.BlockSpec((tm, c), lambda i: (i, 0))]
    if t_cols is not None:
        ct = t_cols[1] - t_cols[0]
        out_shape.append(jax.ShapeDtypeStruct((ct, n), out_dtype))
        out_specs.append(pl.BlockSpec((ct, tm), lambda i: (0, i)))
    return pl.pallas_call(
        functools.partial(_norm_matmul_body, t_cols),
        grid=(n // tm,),
        in_specs=[pl.BlockSpec((tm, d), lambda i: (i, 0)),
                  pl.BlockSpec((1, d), lambda i: (0, 0)),
                  pl.BlockSpec((d, c), lambda i: (0, 0))],
        out_specs=out_specs, out_shape=out_shape,
        name="norm_matmul",
        compiler_params=_cparams(("parallel",)),
    )(x, g, w)


NEG_BIG = -1e30
ATTN_Q_SCALE = DIFF_HEAD_DIM ** -0.5 * math.log2(math.e)


def _diff_attn_body(tq, lam_init, lp_ref, g_ref, q_ref, kt_ref, v_ref, o_ref):
    qi = pl.program_id(2)
    d = DIFF_HEAD_DIM
    lp = lp_ref[...]
    lam = (jnp.exp(jnp.sum(lp[0:1] * lp[1:2], axis=-1, keepdims=True))
           - jnp.exp(jnp.sum(lp[2:3] * lp[3:4], axis=-1, keepdims=True)) + lam_init)
    q = q_ref[...]
    lane = lax.broadcasted_iota(jnp.int32, q.shape, 1)
    q_maps = (jnp.where(lane < d, q, jnp.zeros_like(q)), jnp.where(lane >= d, q, jnp.zeros_like(q)))

    def step(j, carry, masked):
        start = pl.multiple_of(j * tq, tq)
        kt = kt_ref[:, pl.ds(start, tq)]
        vb = v_ref[pl.ds(start, tq), :]
        new = []
        for c in range(2):
            m, l, acc = carry[c]
            s = jnp.dot(q_maps[c], kt, preferred_element_type=F32)
            if masked:
                row = lax.broadcasted_iota(jnp.int32, s.shape, 0)
                col = lax.broadcasted_iota(jnp.int32, s.shape, 1)
                s = jnp.where(col <= row, s, NEG_BIG)
            m_new = jnp.maximum(m, jnp.max(s, axis=-1, keepdims=True))
            alpha = jnp.exp2(m - m_new)
            pr = jnp.exp2(s - m_new)
            l_new = alpha * l + jnp.sum(pr, axis=-1, keepdims=True)
            acc_new = alpha * acc + jnp.dot(pr.astype(BF16), vb, preferred_element_type=F32)
            new.append((m_new, l_new, acc_new))
        return tuple(new)

    init_one = (jnp.full((tq, 1), NEG_BIG, F32), jnp.zeros((tq, 1), F32), jnp.zeros((tq, LANES), F32))
    carry = lax.fori_loop(0, qi, lambda j, c: step(j, c, False), (init_one, init_one))
    (_, l1, acc1), (_, l2, acc2) = step(qi, carry, True)
    o = acc1 / l1 - lam * (acc2 / l2)
    o = o * lax.rsqrt(jnp.mean(o * o, axis=-1, keepdims=True) + SUBLN_EPS)
    o_ref[...] = (o * g_ref[...] * (1.0 - lam_init)).astype(o_ref.dtype)


def diff_attention(qkv, kt, lam_params, subln_g, lam_init, *, tq):
    bsz, seq, c3 = qkv.shape
    n_heads = c3 // (3 * LANES)
    return pl.pallas_call(
        functools.partial(_diff_attn_body, tq, lam_init),
        grid=(bsz, n_heads, seq // tq),
        in_specs=[pl.BlockSpec(lam_params.shape, lambda b, h, i: (0, 0)),
                  pl.BlockSpec((1, LANES), lambda b, h, i: (0, 0)),
                  pl.BlockSpec((None, tq, LANES), lambda b, h, i: (b, i, h)),
                  pl.BlockSpec((LANES, seq), lambda b, h, i: (h, b)),
                  pl.BlockSpec((None, seq, LANES), lambda b, h, i: (b, 0, 2 * n_heads + h))],
        out_specs=pl.BlockSpec((None, tq, LANES), lambda b, h, i: (b, i, h)),
        out_shape=jax.ShapeDtypeStruct((bsz, seq, n_heads * LANES), BF16),
        name="diff_attention",
        compiler_params=_cparams(("parallel", "parallel", "arbitrary")),
    )(lam_params, subln_g, qkv, kt, qkv)


def _out_proj_body(ya_ref, yb_ref, wa_ref, wb_ref, h_ref, o_ref):
    acc = jnp.dot(ya_ref[...], wa_ref[...], preferred_element_type=F32)
    acc = acc + jnp.dot(yb_ref[...], wb_ref[...], preferred_element_type=F32)
    o_ref[...] = h_ref[...] + acc


def out_proj(ya, yb, wa, wb, h, *, tm, tn):
    n, d = h.shape
    ca, cb = ya.shape[1], yb.shape[1]
    return pl.pallas_call(
        _out_proj_body,
        grid=(n // tm, d // tn),
        in_specs=[pl.BlockSpec((tm, ca), lambda i, j: (i, 0)),
                  pl.BlockSpec((tm, cb), lambda i, j: (i, 0)),
                  pl.BlockSpec((ca, tn), lambda i, j: (0, j)),
                  pl.BlockSpec((cb, tn), lambda i, j: (0, j)),
                  pl.BlockSpec((tm, tn), lambda i, j: (i, j))],
        out_specs=pl.BlockSpec((tm, tn), lambda i, j: (i, j)),
        out_shape=jax.ShapeDtypeStruct((n, d), F32),
        name="out_proj",
        compiler_params=_cparams(("parallel", "parallel")),
    )(ya, yb, wa, wb, h)


def _swiglu_tile(u, w1_ref, w3_ref, w2_ref):
    a = jnp.dot(u, w1_ref[...], preferred_element_type=F32)
    b = jnp.dot(u, w3_ref[...], preferred_element_type=F32)
    mid = (a * _sigmoid(a) * b).astype(BF16)
    return jnp.dot(mid, w2_ref[...], preferred_element_type=F32)


def _dense_ffn_body(x_ref, g_ref, w1_ref, w3_ref, w2_ref, o_ref, u_s):
    @pl.when(pl.program_id(1) == 0)
    def _():
        x = x_ref[...]
        u_s[...] = _rmsnorm_rows(x, g_ref[...]).astype(BF16)
        o_ref[...] = x

    o_ref[...] += _swiglu_tile(u_s[...], w1_ref, w3_ref, w2_ref)


def dense_ffn(x, g, w1, w3, w2, *, tm, tf):
    n, d = x.shape
    ff = w1.shape[1]
    assert n % tm == 0 and ff % tf == 0
    return pl.pallas_call(
        _dense_ffn_body,
        grid=(n // tm, ff // tf),
        in_specs=[pl.BlockSpec((tm, d), lambda i, f: (i, 0)),
                  pl.BlockSpec((1, d), lambda i, f: (0, 0)),
                  pl.BlockSpec((d, tf), lambda i, f: (0, f)),
                  pl.BlockSpec((d, tf), lambda i, f: (0, f)),
                  pl.BlockSpec((tf, d), lambda i, f: (f, 0))],
        out_specs=pl.BlockSpec((tm, d), lambda i, f: (i, 0)),
        out_shape=jax.ShapeDtypeStruct((n, d), F32),
        scratch_shapes=[pltpu.VMEM((tm, d), BF16)],
        name="dense_ffn",
        compiler_params=_cparams(("parallel", "arbitrary")),
    )(x, g, w1, w3, w2)


ROUTE_ROWS = 8


def _router_body(n_experts, x_ref, g_ref, wr_ref, u_ref, route_ref):
    u = _rmsnorm_rows(x_ref[...], g_ref[...])
    u_ref[...] = u.astype(BF16)
    u_hi, u_lo = _split2(u)
    wr = wr_ref[...]
    w_hi, w_lo = _split2(wr)
    logits = (jnp.dot(u_hi, w_hi, preferred_element_type=F32) + jnp.dot(u_hi, w_lo, preferred_element_type=F32)
              + jnp.dot(u_lo, w_hi, preferred_element_type=F32))
    lane = lax.broadcasted_iota(jnp.int32, logits.shape, 1)
    logits = jnp.where(lane < n_experts, logits, NEG_BIG)
    top1 = jnp.max(logits, axis=-1, keepdims=True)
    idx1 = jnp.min(jnp.where(logits == top1, lane, LANES), axis=-1, keepdims=True)
    rest = jnp.where(lane == idx1, NEG_BIG, logits)
    top2 = jnp.max(rest, axis=-1, keepdims=True)
    idx2 = jnp.min(jnp.where(rest == top2, lane, LANES), axis=-1, keepdims=True)
    e2 = jnp.exp(top2 - top1)
    gate1 = 1.0 / (1.0 + e2)
    slab = jnp.where(lane == 0, idx1.astype(F32),
                     jnp.where(lane == 1, idx2.astype(F32),
                               jnp.where(lane == 2, gate1, jnp.where(lane == 3, e2 * gate1, 0.0))))
    route_ref[...] = slab.T[:ROUTE_ROWS, :]


def moe_router(x, g, router_w, *, tm):
    n, d = x.shape
    n_experts = router_w.shape[1]
    wr = jnp.zeros((d, LANES), F32).at[:, :n_experts].set(router_w)
    return pl.pallas_call(
        functools.partial(_router_body, n_experts),
        grid=(n // tm,),
        in_specs=[pl.BlockSpec((tm, d), lambda i: (i, 0)),
                  pl.BlockSpec((1, d), lambda i: (0, 0)),
                  pl.BlockSpec((d, LANES), lambda i: (0, 0))],
        out_specs=(pl.BlockSpec((tm, d), lambda i: (i, 0)),
                   pl.BlockSpec((ROUTE_ROWS, tm), lambda i: (0, i))),
        out_shape=(jax.ShapeDtypeStruct((n, d), BF16),
                   jax.ShapeDtypeStruct((ROUTE_ROWS, n), F32)),
        name="moe_router",
        compiler_params=_cparams(("parallel",)),
    )(x, g, wr)


def _moe_ffn_body(be_ref, nu_ref, x_ref, gate_ref, w1_ref, w3_ref, w2_ref, o_ref):
    i, f = pl.program_id(0), pl.program_id(1)
    used = i < nu_ref[0]

    @pl.when(f == 0)
    def _():
        o_ref[...] = jnp.zeros_like(o_ref)

    @pl.when(used)
    def _():
        o_ref[...] += _swiglu_tile(x_ref[...], w1_ref, w3_ref, w2_ref)

    @pl.when(jnp.logical_and(used, f == pl.num_programs(1) - 1))
    def _():
        o_ref[...] = o_ref[...] * gate_ref[...]


def moe_ffn(x_rows, row_gate, blk_expert, n_used, w1, w3, w2, layer, *, tm, tf):
    rows, d = x_rows.shape
    ff = w1.shape[3]
    nf = ff // tf

    def f_idx(i, f, nu):
        return jnp.where(i < nu[0], f, nf - 1)

    grid_spec = pltpu.PrefetchScalarGridSpec(
        num_scalar_prefetch=2,
        grid=(rows // tm, nf),
        in_specs=[pl.BlockSpec((tm, d), lambda i, f, be, nu: (i, 0)),
                  pl.BlockSpec((tm, 1), lambda i, f, be, nu: (i, 0)),
                  pl.BlockSpec((None, None, d, tf), lambda i, f, be, nu: (layer, be[i], 0, f_idx(i, f, nu))),
                  pl.BlockSpec((None, None, d, tf), lambda i, f, be, nu: (layer, be[i], 0, f_idx(i, f, nu))),
                  pl.BlockSpec((None, None, tf, d), lambda i, f, be, nu: (layer, be[i], f_idx(i, f, nu), 0))],
        out_specs=pl.BlockSpec((tm, d), lambda i, f, be, nu: (i, 0)),
    )
    return pl.pallas_call(
        _moe_ffn_body, grid_spec=grid_spec,
        out_shape=jax.ShapeDtypeStruct((rows, d), F32),
        name="moe_ffn",
        compiler_params=_cparams(("arbitrary", "arbitrary"), MOE_VMEM_LIMIT),
    )(blk_expert, n_used, x_rows, row_gate, w1, w3, w2)


def moe_layer(h, g, router_w, w1, w3, w2, layer, *, tm_router, tm, tf):
    n, d = h.shape
    n_experts = router_w.shape[1]
    u, route = moe_router(h, g, router_w, tm=tm_router)
    e_flat = route[:TOP_K].T.reshape(-1).astype(jnp.int32)
    g_flat = route[TOP_K:2 * TOP_K].T.reshape(-1)
    m = n * TOP_K
    onehot = (e_flat[:, None] == jnp.arange(n_experts, dtype=jnp.int32)[None, :]).astype(jnp.int32)
    counts = jnp.sum(onehot, axis=0)
    rank = jnp.sum((jnp.cumsum(onehot, axis=0) - onehot) * onehot, axis=1)
    padded = (counts + tm - 1) // tm * tm
    pad_ends = jnp.cumsum(padded)
    dest = (pad_ends - padded)[e_flat] + rank
    n_blocks = m // tm + n_experts
    rows = n_blocks * tm
    row_slot = jnp.full((rows,), -1, jnp.int32).at[dest].set(jnp.arange(m, dtype=jnp.int32))
    row_tok = jnp.maximum(row_slot, 0) // TOP_K
    row_gate = jnp.where(row_slot >= 0, g_flat[jnp.maximum(row_slot, 0)], 0.0)
    blk_start = jnp.arange(n_blocks, dtype=jnp.int32) * tm
    blk_expert = jnp.minimum(jnp.sum(blk_start[:, None] >= pad_ends[None, :], axis=1), n_experts - 1).astype(jnp.int32)
    n_used = (pad_ends[-1:] // tm).astype(jnp.int32)
    y_rows = moe_ffn(u[row_tok], row_gate[:, None], blk_expert, n_used, w1, w3, w2, layer, tm=tm, tf=tf)
    dest = dest.reshape(n, TOP_K)
    return [y_rows[dest[:, k]] for k in range(TOP_K)]


def _ple_body(final_norm, n_add, x_ref, *refs):
    g_ref, wg_ref, p_ref, wp_ref, gf_ref, o_ref = refs[n_add:]
    x = x_ref[...]
    for add_ref in refs[:n_add]:
        x = x + add_ref[...]
    u = _rmsnorm_rows(x, g_ref[...]).astype(BF16)
    gate = _sigmoid(jnp.dot(u, wg_ref[...], preferred_element_type=F32))
    proj = jnp.dot(p_ref[...].astype(BF16), wp_ref[...], preferred_element_type=F32)
    out = x + gate * proj
    if final_norm:
        out = _rmsnorm_rows(out, gf_ref[...])
    o_ref[...] = out


def ple_layer(x, addends, g, w_gate, p, w_proj, g_final, final_norm, *, tm):
    n, d = x.shape
    dp = p.shape[1]
    add_args = list(addends)
    add_specs = [pl.BlockSpec((tm, d), lambda i: (i, 0)) for _ in add_args]
    return pl.pallas_call(
        functools.partial(_ple_body, final_norm, len(add_args)),
        grid=(n // tm,),
        in_specs=[pl.BlockSpec((tm, d), lambda i: (i, 0))] + add_specs + [
                  pl.BlockSpec((1, d), lambda i: (0, 0)),
                  pl.BlockSpec((d, d), lambda i: (0, 0)),
                  pl.BlockSpec((tm, dp), lambda i: (i, 0)),
                  pl.BlockSpec((dp, d), lambda i: (0, 0)),
                  pl.BlockSpec((1, d), lambda i: (0, 0))],
        out_specs=pl.BlockSpec((tm, d), lambda i: (i, 0)),
        out_shape=jax.ShapeDtypeStruct((n, d), F32),
        name="ple_layer",
        compiler_params=_cparams(("parallel",)),
    )(x, *add_args, g, w_gate, p, w_proj, g_final)


def _lambda_init(layer_idx):
    return 0.8 - 0.6 * math.exp(-0.3 * layer_idx)


def _tile(n, pref):
    t = min(n, pref)
    while n % t:
        t //= 2
    return t


def kernel(x, p, g_mix, w_in, w_vres_down, mu_shift, mu_vres, rw_w0, rw_w2, rw_a0, rw_a2, rw_g2, rw_v0, rw_v2, rw_k_k, rw_k_a, rw_r_k, rw_lnx_w, rw_lnx_b, diff_lambda, diff_subln, w_out, g_ffn, dense_w1, dense_w3, dense_w2, router_w, moe_w1, moe_w3, moe_w2, g_ple, w_ple_gate, w_ple_proj, g_final):
    bsz, seq, d = x.shape
    depth = w_in.shape[0]
    n = bsz * seq
    c_rwkv = rw_w0.shape[1]
    shift_cols = mu_shift.shape[1]
    zr_cols = 3 * c_rwkv + LORA_COLS
    row = lambda v: v.reshape(1, -1)
    tm = _tile(n, 1024)
    h = x.reshape(n, d)
    moe_w = (moe_w1.astype(BF16), moe_w3.astype(BF16), moe_w2.astype(BF16))
    v_first = None
    for i in range(depth):
        w_vres = w_vres_down[i - 1] if i > 0 else jnp.zeros((d, D_MV_LORA), F32)
        w_r = jnp.concatenate([w_in[i][:, :shift_cols], w_vres,
                               jnp.zeros((d, zr_cols - shift_cols - D_MV_LORA), F32)], axis=1).astype(BF16)
        w_d = w_in[i][:, shift_cols:]
        c_diff = w_d.shape[1] // 3
        w_d = jnp.concatenate([w_d[:, :c_diff] * ATTN_Q_SCALE, w_d[:, c_diff:]], axis=1).astype(BF16)
        z_r, = norm_matmul(h, row(g_mix[i]), w_r, F32, tm=_tile(n, 512))
        qkv, kt = norm_matmul(h, row(g_mix[i]), w_d, BF16, (c_diff, 2 * c_diff), tm=_tile(n, 512))
        packed = _rwkv_params(i, mu_shift, mu_vres, rw_w0, rw_w2, rw_a0, rw_a2, rw_g2, rw_v0, rw_v2,
                              rw_k_k, rw_k_a, rw_r_k, rw_lnx_w, rw_lnx_b)
        mixed = rwkv_time_mix(z_r.reshape(bsz, seq, zr_cols), *packed, v_first, tb=_tile(seq, 512), nt=4)
        if i == 0:
            y_rwkv, v_first = mixed
        else:
            y_rwkv = mixed
        y_diff = diff_attention(qkv.reshape(bsz, seq, -1), kt, diff_lambda[i], row(diff_subln[i]), _lambda_init(i),
                                tq=_tile(seq, 1024))
        wo = w_out[i].astype(BF16)
        h = out_proj(y_rwkv.reshape(n, c_rwkv), y_diff.reshape(n, -1), wo[:c_rwkv], wo[c_rwkv:], h,
                     tm=_tile(n, 512), tn=d)
        expert_out = []
        if i % 2 == 0:
            j = i // 2
            h = dense_ffn(h, row(g_ffn[i]), dense_w1[j].astype(BF16), dense_w3[j].astype(BF16),
                          dense_w2[j].astype(BF16), tm=_tile(n, 1024), tf=512)
        else:
            expert_out = moe_layer(h, row(g_ffn[i]), router_w[i // 2], *moe_w, i // 2,
                                   tm_router=_tile(n, 512), tm=_tile(n, 1024), tf=1024)
        h = ple_layer(h, expert_out, row(g_ple[i]), w_ple_gate[i].astype(BF16), p[i].reshape(n, -1),
                      w_ple_proj[i].astype(BF16), row(g_final), i == depth - 1,
                      tm=_tile(n, 256 if expert_out else 512))
    return h.reshape(bsz, seq, d)
```
